```python
import math
import jax, jax.numpy as jnp
from jax import lax
import numpy as np

D_MODEL = 2048
BATCH = 4
SEQ = 2048
DEPTH = 1

D_MIX = D_MODEL
SSD_WIDTH = D_MIX // 2
SSD_HEAD_DIM = 64
SSD_HEADS = SSD_WIDTH // SSD_HEAD_DIM
SSD_GROUPS = 2
SSD_HPG = SSD_HEADS // SSD_GROUPS
SSD_STATE = 128
SSD_CONV = 4
SSD_CHUNK = 128
XBC_WIDTH = SSD_WIDTH + 2 * SSD_GROUPS * SSD_STATE
DA_WIDTH = D_MIX - SSD_WIDTH
DA_HEAD_DIM = 128
DA_HALF = DA_HEAD_DIM // 2
DA_HEADS = DA_WIDTH // DA_HEAD_DIM
Q_BLOCK = 128
IN_COLS = SSD_WIDTH + XBC_WIDTH + SSD_HEADS + 3 * DA_WIDTH
MEM_LEN = 256
XA_HEADS = 4
XA_HEAD_DIM = D_MODEL // XA_HEADS
N_EXPERTS = 32
TOP_K = 4
D_FF = D_MODEL
SWIGLU_LIMIT = 7.0
SWIGLU_ALPHA = 1.702
MOE_BLOCK = 256
EPS = 1e-6

kernel_name = 'hymba_ssd_diffattn_xmem_moe_layer'


def rms_norm(u, g):
    uf = u.astype(jnp.float32)
    uf = uf * lax.rsqrt(jnp.mean(uf * uf, axis=-1, keepdims=True) + EPS)
    return uf.astype(u.dtype) * g


def causal_depthwise_conv(u, w, b):
    out = lax.conv_general_dilated(
        u, w[:, None, :].astype(u.dtype), window_strides=(1,),
        padding=[(SSD_CONV - 1, 0)], dimension_numbers=('NWC', 'WIO', 'NWC'),
        feature_group_count=u.shape[-1])
    return out + b


def ssd_chunked_scan(xh, dt, a, bm, cm):
    bsz, s = xh.shape[0], xh.shape[1]
    nc, L = s // SSD_CHUNK, SSD_CHUNK
    f32 = jnp.float32
    x = (xh.astype(f32) * dt[..., None]).reshape(bsz, nc, L, SSD_GROUPS, SSD_HPG, SSD_HEAD_DIM)
    adt = (dt * a).reshape(bsz, nc, L, SSD_GROUPS, SSD_HPG)
    bmc = bm.astype(f32).reshape(bsz, nc, L, SSD_GROUPS, SSD_STATE)
    cmc = cm.astype(f32).reshape(bsz, nc, L, SSD_GROUPS, SSD_STATE)
    acum = jnp.cumsum(adt, axis=2)
    seg = acum[:, :, :, None] - acum[:, :, None, :]
    causal = jnp.tril(jnp.ones((L, L), dtype=bool))[None, None, :, :, None, None]
    decay = jnp.exp(jnp.where(causal, seg, -jnp.inf))
    cb = jnp.einsum('bclgn,bcsgn->bcgls', cmc, bmc)
    y_diag = jnp.einsum('bcgls,bclsgr,bcsgrp->bclgrp', cb, decay, x)
    decay_to_end = jnp.exp(acum[:, :, -1:] - acum)
    states = jnp.einsum('bclgn,bclgr,bclgrp->bcgrpn', bmc, decay_to_end, x)
    chunk_decay = jnp.exp(acum[:, :, -1])

    def step(h, inp):
        st, dec = inp
        return h * dec[..., None, None] + st, h

    h0 = jnp.zeros((bsz, SSD_GROUPS, SSD_HPG, SSD_HEAD_DIM, SSD_STATE), f32)
    _, h_in = lax.scan(step, h0, (jnp.moveaxis(states, 1, 0), jnp.moveaxis(chunk_decay, 1, 0)))
    h_in = jnp.moveaxis(h_in, 0, 1)
    y_off = jnp.einsum('bclgn,bcgrpn,bclgr->bclgrp', cmc, h_in, jnp.exp(acum))
    return (y_diag + y_off).reshape(bsz, s, SSD_GROUPS, SSD_HPG, SSD_HEAD_DIM)


def ssd_group(z, xbc, dt_raw, conv_w, conv_b, dt_bias, a_log, d_skip, norm_g):
    bsz, s = z.shape[0], z.shape[1]
    xbc = jax.nn.silu(causal_depthwise_conv(xbc, conv_w, conv_b))
    xs = xbc[..., :SSD_WIDTH].reshape(bsz, s, SSD_GROUPS, SSD_HPG, SSD_HEAD_DIM)
    bm = xbc[..., SSD_WIDTH:SSD_WIDTH + SSD_GROUPS * SSD_STATE].reshape(bsz, s, SSD_GROUPS, SSD_STATE)
    cm = xbc[..., SSD_WIDTH + SSD_GROUPS * SSD_STATE:].reshape(bsz, s, SSD_GROUPS, SSD_STATE)
    dt = jax.nn.softplus(dt_raw.astype(jnp.float32) + dt_bias.astype(jnp.float32))
    dt = dt.reshape(bsz, s, SSD_GROUPS, SSD_HPG)
    a = -jnp.exp(a_log.astype(jnp.float32)).reshape(SSD_GROUPS, SSD_HPG)
    y = ssd_chunked_scan(xs, dt, a, bm, cm)
    y = y + xs.astype(jnp.float32) * d_skip.astype(jnp.float32).reshape(SSD_GROUPS, SSD_HPG)[..., None]
    y = y.astype(z.dtype).reshape(bsz, s, SSD_WIDTH) * jax.nn.silu(z)
    y = rms_norm(y.reshape(bsz, s, SSD_GROUPS, SSD_WIDTH // SSD_GROUPS),
                 jnp.ones((), y.dtype)).reshape(bsz, s, SSD_WIDTH) * norm_g
    return y


def diff_attention_group(q, k, v, q_norm_g, k_norm_g, lq1, lk1, lq2, lk2, subln_g, lam_init):
    bsz, s = q.shape[0], q.shape[1]
    q = rms_norm(q.reshape(bsz, s, DA_HEADS, 2, DA_HALF), q_norm_g) * (DA_HALF ** -0.5)
    k = rms_norm(k.reshape(bsz, s, DA_HEADS, 2, DA_HALF), k_norm_g)
    v = v.reshape(bsz, s, DA_HEADS, DA_HEAD_DIM)
    f32 = jnp.float32
    lam = (jnp.exp(jnp.sum(lq1.astype(f32) * lk1.astype(f32)))
           - jnp.exp(jnp.sum(lq2.astype(f32) * lk2.astype(f32))) + lam_init)
    outs = []
    for i in range(s // Q_BLOCK):
        start, end = i * Q_BLOCK, (i + 1) * Q_BLOCK
        sc = jnp.einsum('bqhjd,bkhjd->bhjqk', q[:, start:end], k[:, :end],
                        preferred_element_type=f32)
        mask = jnp.arange(end)[None, :] <= (start + jnp.arange(Q_BLOCK))[:, None]
        p = jax.nn.softmax(jnp.where(mask, sc, -jnp.inf), axis=-1)
        w = p[:, :, 0] - lam * p[:, :, 1]
        outs.append(jnp.einsum('bhqk,bkhe->bqhe', w.astype(v.dtype), v[:, :end]))
    o = jnp.concatenate(outs, axis=1)
    o = rms_norm(o, subln_g) * (1.0 - lam_init)
    return o.reshape(bsz, s, DA_WIDTH)


def memory_cross_attention(hn, mem, norm_mem_g, wq, wkv, q_norm_g, k_norm_g, wo):
    bsz, s = hn.shape[0], hn.shape[1]
    mn = rms_norm(mem, norm_mem_g)
    q = rms_norm((hn @ wq).reshape(bsz, s, XA_HEADS, XA_HEAD_DIM), q_norm_g)
    kv = mn @ wkv
    k = rms_norm(kv[..., :D_MODEL].reshape(bsz, MEM_LEN, XA_HEADS, XA_HEAD_DIM), k_norm_g)
    v = kv[..., D_MODEL:].reshape(bsz, MEM_LEN, XA_HEADS, XA_HEAD_DIM)
    sc = jnp.einsum('bqhd,bkhd->bhqk', q, k, preferred_element_type=jnp.float32) * (XA_HEAD_DIM ** -0.5)
    p = jax.nn.softmax(sc, axis=-1)
    o = jnp.einsum('bhqk,bkhd->bqhd', p.astype(v.dtype), v).reshape(bsz, s, D_MODEL)
    return o @ wo


def moe_ffn(hn, router_w, router_b, w1, b1, w2, b2):
    bsz, s, d = hn.shape
    t = bsz * s
    hf = hn.reshape(t, d)
    logits = (hf @ router_w).astype(jnp.float32) + router_b.astype(jnp.float32)
    top_val, top_idx = lax.top_k(logits, TOP_K)
    gates = jax.nn.softmax(top_val, axis=-1)
    tk = t * TOP_K
    flat_e = top_idx.reshape(-1)
    flat_tok = jnp.repeat(jnp.arange(t, dtype=jnp.int32), TOP_K)
    order = jnp.argsort(flat_e)
    sorted_e = flat_e[order]
    sorted_tok = flat_tok[order]
    sorted_gate = gates.reshape(-1)[order]
    counts = jnp.bincount(flat_e, length=N_EXPERTS)
    padded = ((counts + MOE_BLOCK - 1) // MOE_BLOCK) * MOE_BLOCK
    start = jnp.cumsum(counts) - counts
    padded_end = jnp.cumsum(padded)
    padded_start = padded_end - padded
    dest = padded_start[sorted_e] + (jnp.arange(tk) - start[sorted_e])
    n_blocks = -(-tk // MOE_BLOCK) + N_EXPERTS
    n_slots = n_blocks * MOE_BLOCK
    slot_tok = jnp.full((n_slots,), t, jnp.int32).at[dest].set(sorted_tok)
    slot_gate = jnp.zeros((n_slots,), jnp.float32).at[dest].set(sorted_gate)
    block_expert = jnp.clip(jnp.searchsorted(padded_end, jnp.arange(n_blocks) * MOE_BLOCK, side='right'),
                            0, N_EXPERTS - 1)
    hf_pad = jnp.concatenate([hf, jnp.zeros((1, d), hf.dtype)], axis=0)
    xs = hf_pad[slot_tok].reshape(n_blocks, MOE_BLOCK, d)

    def expert_block(args):
        xb, e = args
        gu = xb @ w1[e] + b1[e]
        g = jnp.minimum(gu[:, :D_FF], SWIGLU_LIMIT)
        u = jnp.clip(gu[:, D_FF:], -SWIGLU_LIMIT, SWIGLU_LIMIT)
        act = g * jax.nn.sigmoid(SWIGLU_ALPHA * g) * (u + 1.0)
        return act @ w2[e] + b2[e]

    y = lax.map(expert_block, (xs, block_expert)).reshape(n_slots, d)
    y = y.astype(jnp.float32) * slot_gate[:, None]
    out = jnp.zeros((t + 1, d), jnp.float32).at[slot_tok].add(y)[:t]
    return out.astype(hn.dtype).reshape(bsz, s, d)


def setup_inputs(seed: int = 0) -> dict:
    key = jax.random.key(seed)
    ks = jax.random.split(key, 40)
    f32 = jnp.float32
    L = DEPTH

    def nrm(k, shape, scale):
        return jax.random.normal(k, shape, f32) * scale

    def gain(k, shape):
        return 1.0 + 0.02 * jax.random.normal(k, shape, f32)

    dt0 = jnp.exp(jax.random.uniform(ks[6], (L, SSD_HEADS), f32) * (math.log(0.1) - math.log(0.001)) + math.log(0.001))
    return {
        'x': nrm(ks[0], (BATCH, SEQ, D_MODEL), 1.0),
        'mem': nrm(ks[1], (BATCH, MEM_LEN, D_MODEL), 1.0),
        'norm_mix_g': gain(ks[2], (L, D_MODEL)),
        'w_in': nrm(ks[3], (L, D_MODEL, IN_COLS), D_MODEL ** -0.5),
        'conv_w': nrm(ks[4], (L, SSD_CONV, XBC_WIDTH), SSD_CONV ** -0.5),
        'conv_b': nrm(ks[5], (L, XBC_WIDTH), 0.02),
        'dt_bias': dt0 + jnp.log(-jnp.expm1(-dt0)),
        'a_log': jnp.log(jax.random.uniform(ks[7], (L, SSD_HEADS), f32, 1.0, 16.0)),
        'd_skip': gain(ks[8], (L, SSD_HEADS)),
        'ssd_norm_g': gain(ks[9], (L, SSD_WIDTH)),
        'da_q_norm_g': gain(ks[10], (L, DA_HALF)),
        'da_k_norm_g': gain(ks[11], (L, DA_HALF)),
        'lambda_q1': nrm(ks[12], (L, DA_HALF), 0.1),
        'lambda_k1': nrm(ks[13], (L, DA_HALF), 0.1),
        'lambda_q2': nrm(ks[14], (L, DA_HALF), 0.1),
        'lambda_k2': nrm(ks[15], (L, DA_HALF), 0.1),
        'da_subln_g': gain(ks[16], (L, DA_HEAD_DIM)),
        'w_out': nrm(ks[17], (L, D_MIX, D_MODEL), D_MIX ** -0.5),
        'norm_xa_g': gain(ks[18], (L, D_MODEL)),
        'norm_mem_g': gain(ks[19], (L, D_MODEL)),
        'xa_wq': nrm(ks[20], (L, D_MODEL, D_MODEL), D_MODEL ** -0.5),
        'xa_wkv': nrm(ks[21], (L, D_MODEL, 2 * D_MODEL), D_MODEL ** -0.5),
        'xa_q_norm_g': gain(ks[22], (L, XA_HEAD_DIM)),
        'xa_k_norm_g': gain(ks[23], (L, XA_HEAD_DIM)),
        'xa_wo': nrm(ks[24], (L, D_MODEL, D_MODEL), D_MODEL ** -0.5),
        'norm_ffn_g': gain(ks[25], (L, D_MODEL)),
        'router_w': nrm(ks[26], (L, D_MODEL, N_EXPERTS), D_MODEL ** -0.5),
        'router_b': nrm(ks[27], (L, N_EXPERTS), 0.01),
        'moe_w1': nrm(ks[28], (L, N_EXPERTS, D_MODEL, 2 * D_FF), D_MODEL ** -0.5),
        'moe_b1': nrm(ks[29], (L, N_EXPERTS, 2 * D_FF), 0.01),
        'moe_w2': nrm(ks[30], (L, N_EXPERTS, D_FF, D_MODEL), D_FF ** -0.5),
        'moe_b2': nrm(ks[31], (L, N_EXPERTS, D_MODEL), 0.01),
    }


def reference(x, mem, norm_mix_g, w_in, conv_w, conv_b, dt_bias, a_log, d_skip, ssd_norm_g,
              da_q_norm_g, da_k_norm_g, lambda_q1, lambda_k1, lambda_q2, lambda_k2, da_subln_g,
              w_out, norm_xa_g, norm_mem_g, xa_wq, xa_wkv, xa_q_norm_g, xa_k_norm_g, xa_wo,
              norm_ffn_g, router_w, router_b, moe_w1, moe_b1, moe_w2, moe_b2):
    h = x
    for layer in range(DEPTH):
        lam_init = 0.8 - 0.6 * math.exp(-0.3 * layer)
        hn = rms_norm(h, norm_mix_g[layer])
        proj = hn @ w_in[layer]
        o0 = SSD_WIDTH
        o1 = o0 + XBC_WIDTH
        o2 = o1 + SSD_HEADS
        o3 = o2 + DA_WIDTH
        o4 = o3 + DA_WIDTH
        y_ssd = ssd_group(proj[..., :o0], proj[..., o0:o1], proj[..., o1:o2],
                          conv_w[layer], conv_b[layer], dt_bias[layer], a_log[layer],
                          d_skip[layer], ssd_norm_g[layer])
        y_da = diff_attention_group(proj[..., o2:o3], proj[..., o3:o4], proj[..., o4:],
                                    da_q_norm_g[layer], da_k_norm_g[layer],
                                    lambda_q1[layer], lambda_k1[layer], lambda_q2[layer], lambda_k2[layer],
                                    da_subln_g[layer], lam_init)
        h = h + jnp.concatenate([y_ssd, y_da], axis=-1) @ w_out[layer]
        h = h + memory_cross_attention(rms_norm(h, norm_xa_g[layer]), mem, norm_mem_g[layer],
                                       xa_wq[layer], xa_wkv[layer], xa_q_norm_g[layer],
                                       xa_k_norm_g[layer], xa_wo[layer])
        h = h + moe_ffn(rms_norm(h, norm_ffn_g[layer]), router_w[layer], router_b[layer],
                        moe_w1[layer], moe_b1[layer], moe_w2[layer], moe_b2[layer])
    return h
```

```python
import functools
import math

import jax
import jax.numpy as jnp
from jax import lax
from jax.experimental import pallas as pl
from jax.experimental.pallas import tpu as pltpu

F32 = jnp.float32
BF16 = jnp.bfloat16
I32 = jnp.int32

D_MODEL = 2048
SSD_WIDTH = 1024
SSD_HEAD_DIM = 64
SSD_HEADS = 16
SSD_GROUPS = 2
SSD_STATE = 128
SSD_CONV = 4
SSD_CHUNK = 128
XBC_WIDTH = SSD_WIDTH + 2 * SSD_GROUPS * SSD_STATE
DA_WIDTH = 1024
DA_HEAD_DIM = 128
DA_HALF = 64
DA_HEADS = 8
XA_HEADS = 4
XA_HEAD_DIM = D_MODEL // XA_HEADS
N_EXPERTS = 32
TOP_K = 4
D_FF = 2048
SWIGLU_LIMIT = 7.0
SWIGLU_ALPHA = 1.702
EPS = 1e-6

LANES = 128
ROW_SUB = D_MODEL // LANES
VMEM_LIMIT = 56 * 1024 * 1024

PROJ_COLS = SSD_WIDTH + XBC_WIDTH + 3 * DA_WIDTH
COL_Q = (SSD_WIDTH + XBC_WIDTH) // LANES
COL_V = COL_Q + 2 * DA_HEADS

MOE_TM = 512
MOE_SUB = 256
MOE_TF = 1024


def _dot(a, b):
    return jnp.dot(a, b, preferred_element_type=F32)


def _dot_nt(a, b):
    return lax.dot_general(a, b, (((1,), (1,)), ((), ())), preferred_element_type=F32)


def _dot_tn(a, b):
    return lax.dot_general(a, b, (((0,), (0,)), ((), ())), preferred_element_type=F32)


def _rms(x):
    return x * lax.rsqrt(jnp.mean(x * x, axis=-1, keepdims=True) + EPS)


def _split2(x):
    hi = x.astype(BF16)
    mid = (x - hi.astype(F32)).astype(BF16)
    return hi, mid


def _split3(x):
    hi = x.astype(BF16)
    r = x - hi.astype(F32)
    mid = r.astype(BF16)
    lo = (r - mid.astype(F32)).astype(BF16)
    return hi, mid, lo


def _split3_host(w):
    hi, mid, lo = _split3(w.astype(F32))
    return jnp.stack([hi, mid, lo])


def _silu(x):
    return x * jax.nn.sigmoid(x)


def _cparams(sem):
    return pltpu.CompilerParams(dimension_semantics=sem, vmem_limit_bytes=VMEM_LIMIT)


def _in_proj_kernel(x_ref, g_ref, w_ref, wdt_ref, o_ref, dt_ref, xn_ref):
    @pl.when(pl.program_id(1) == 0)
    def _():
        hn = _rms(x_ref[...]) * g_ref[...]
        hi, mid = _split2(hn)
        xn_ref[...] = hi
        dt_ref[...] = _dot(hi, wdt_ref[0]) + (_dot(hi, wdt_ref[1]) + _dot(mid, wdt_ref[0]))

    o_ref[...] = _dot(xn_ref[...], w_ref[...]).astype(o_ref.dtype)


def _in_proj(x2, g, w, wdt3, tm=1024, tn=512):
    m, k = x2.shape
    n = w.shape[1]
    return pl.pallas_call(
        _in_proj_kernel,
        grid=(m // tm, n // tn),
        in_specs=[
            pl.BlockSpec((tm, k), lambda i, j: (i, 0)),
            pl.BlockSpec((1, k), lambda i, j: (0, 0)),
            pl.BlockSpec((k, tn), lambda i, j: (0, j)),
            pl.BlockSpec((3, k, LANES), lambda i, j: (0, 0, 0)),
        ],
        out_specs=[
            pl.BlockSpec((tm, tn), lambda i, j: (i, j)),
            pl.BlockSpec((tm, LANES), lambda i, j: (i, 0)),
        ],
        out_shape=[jax.ShapeDtypeStruct((m, n), BF16), jax.ShapeDtypeStruct((m, LANES), F32)],
        scratch_shapes=[pltpu.VMEM((tm, k), BF16)],
        compiler_params=_cparams(("parallel", "arbitrary")),
        name="in_proj",
    )(x2, g, w, wdt3)


def _norm_proj_kernel(x_ref, g_ref, w_ref, eg_ref, o_ref, xn_ref, *, epi_tiles):
    j = pl.program_id(1)

    @pl.when(j == 0)
    def _():
        xn_ref[...] = (_rms(x_ref[...]) * g_ref[...]).astype(BF16)

    acc = _dot(xn_ref[...], w_ref[...])

    @pl.when(j < epi_tiles)
    def _():
        o_ref[...] = (_rms(acc) * eg_ref[...]).astype(o_ref.dtype)

    @pl.when(j >= epi_tiles)
    def _():
        o_ref[...] = acc.astype(o_ref.dtype)


def _norm_proj(x2, g, w, eg, epi_tiles, tm, tn):
    m, k = x2.shape
    n = w.shape[1]
    return pl.pallas_call(
        functools.partial(_norm_proj_kernel, epi_tiles=epi_tiles),
        grid=(m // tm, n // tn),
        in_specs=[
            pl.BlockSpec((tm, k), lambda i, j: (i, 0)),
            pl.BlockSpec((1, k), lambda i, j: (0, 0)),
            pl.BlockSpec((k, tn), lambda i, j: (0, j)),
            pl.BlockSpec((1, tn), lambda i, j: (0, 0)),
        ],
        out_specs=pl.BlockSpec((tm, tn), lambda i, j: (i, j)),
        out_shape=jax.ShapeDtypeStruct((m, n), BF16),
        scratch_shapes=[pltpu.VMEM((tm, k), BF16)],
        compiler_params=_cparams(("parallel", "arbitrary")),
        name="norm_proj",
    )(x2, g, w, eg)


def _mm_res_kernel(*refs, n_a):
    a_refs, w_refs = refs[:n_a], refs[n_a:2 * n_a]
    res_ref, o_ref = refs[2 * n_a], refs[2 * n_a + 1]
    acc = res_ref[...]
    for a, w in zip(a_refs, w_refs):
        acc = acc + _dot(a[...], w[...])
    o_ref[...] = acc


def _mm_res(a_list, w, res, tm=1024, tn=512):
    m, n = res.shape
    n_a = len(a_list)
    in_specs = []
    for a in a_list:
        in_specs.append(pl.BlockSpec((tm, a.shape[1]), lambda i, j: (i, 0)))
    for idx, a in enumerate(a_list):
        in_specs.append(pl.BlockSpec((a.shape[1], tn), lambda i, j, idx=idx: (idx, j)))
    in_specs.append(pl.BlockSpec((tm, tn), lambda i, j: (i, j)))
    return pl.pallas_call(
        functools.partial(_mm_res_kernel, n_a=n_a),
        grid=(m // tm, n // tn),
        in_specs=in_specs,
        out_specs=pl.BlockSpec((tm, tn), lambda i, j: (i, j)),
        out_shape=jax.ShapeDtypeStruct((m, n), F32),
        compiler_params=_cparams(("parallel", "arbitrary")),
        name="mm_res",
    )(*a_list, *([w] * n_a), res)


def _ssd_kernel(z_ref, xs_ref, b_ref, c_ref, dtr_ref, cw_ref, cb_ref, dtb_ref, alog_ref,
                dsk_ref, ng_ref, sel_ref, o_ref, ext_ref, ht_ref, y_ref):
    L = SSD_CHUNK
    n_pairs = SSD_HEADS // 2
    pairs_per_group = n_pairs // SSD_GROUPS
    c_off = SSD_WIDTH + SSD_GROUPS * SSD_STATE

    @pl.when(pl.program_id(1) == 0)
    def _():
        ht_ref[...] = jnp.zeros_like(ht_ref)
        ext_ref[0:8, :] = jnp.zeros((8, XBC_WIDTH), F32)

    ext_ref[8:8 + L, 0:SSD_WIDTH] = xs_ref[0].astype(F32)
    ext_ref[8:8 + L, SSD_WIDTH:c_off] = b_ref[0].astype(F32)
    ext_ref[8:8 + L, c_off:XBC_WIDTH] = c_ref[0].astype(F32)
    acc = jnp.broadcast_to(cb_ref[...], (L, XBC_WIDTH))
    for k in range(SSD_CONV):
        acc = acc + cw_ref[k:k + 1, :] * ext_ref[pl.ds(8 - (SSD_CONV - 1) + k, L), :]
    ext_ref[0:8, :] = ext_ref[L:L + 8, :]
    xbc = _silu(acc)

    dt = jax.nn.softplus(dtr_ref[0] + dtb_ref[...])
    a = -jnp.exp(alog_ref[...])
    adt = dt * a
    row = lax.broadcasted_iota(I32, (L, L), 0)
    col = lax.broadcasted_iota(I32, (L, L), 1)
    causal = col <= row
    tril = jnp.where(causal, 1.0, 0.0).astype(BF16)
    a_hi, a_mid, a_lo = _split3(adt)
    acum = _dot(tril, a_hi) + (_dot(tril, a_mid) + _dot(tril, a_lo))
    acum_t = acum.T
    pieces = [jnp.concatenate([p, q], axis=0) for p, q in zip(_split3(acum), _split3(dt))]

    lane = lax.broadcasted_iota(I32, (L, LANES), 1)
    first = lane < SSD_HEAD_DIM

    cbs = []
    for g in range(SSD_GROUPS):
        bg = xbc[:, SSD_WIDTH + g * SSD_STATE:SSD_WIDTH + (g + 1) * SSD_STATE].astype(BF16)
        cg = xbc[:, c_off + g * SSD_STATE:c_off + (g + 1) * SSD_STATE].astype(BF16)
        cbs.append((bg, cg, _dot_nt(cg, bg)))

    for p in range(n_pairs):
        bg, cg, cb = cbs[p // pairs_per_group]
        sel = sel_ref[:, 2 * LANES * p:2 * LANES * (p + 1)]
        fb = _dot(pieces[0], sel) + (_dot(pieces[1], sel) + _dot(pieces[2], sel))
        ms = []
        for i in range(2):
            h = 2 * p + i
            seg = fb[:L, i * LANES:(i + 1) * LANES] - acum_t[h:h + 1, :]
            dec = jnp.exp(jnp.where(causal, seg, -jnp.inf))
            ms.append((cb * dec).astype(BF16))
        ac = jnp.where(first, fb[:L, :LANES], fb[:L, LANES:])
        dtp = jnp.where(first, fb[L:, :LANES], fb[L:, LANES:])
        xp = xbc[:, p * LANES:(p + 1) * LANES]
        xdt = xp * dtp
        y = _dot(ms[0], jnp.where(first, xdt, 0.0).astype(BF16))
        y = y + _dot(ms[1], jnp.where(first, 0.0, xdt).astype(BF16))
        ht = ht_ref[p]
        y = y + _dot(cg, ht.astype(BF16)) * jnp.exp(ac)
        a_last = ac[L - 1:L, :]
        ht_ref[p] = ht * jnp.exp(a_last) + _dot_tn(bg, (xdt * jnp.exp(a_last - ac)).astype(BF16))
        y_ref[:, p * LANES:(p + 1) * LANES] = y + xp * dsk_ref[:, p * LANES:(p + 1) * LANES]

    yg = y_ref[...] * _silu(z_ref[0].astype(F32))
    gw = SSD_WIDTH // SSD_GROUPS
    for g in range(SSD_GROUPS):
        blk = _rms(yg[:, g * gw:(g + 1) * gw]) * ng_ref[:, g * gw:(g + 1) * gw]
        o_ref[0, :, g * gw:(g + 1) * gw] = blk.astype(o_ref.dtype)


def _ssd(proj3, dt3, conv_w, conv_b, dt_bias, a_log, d_skip, norm_g):
    bsz, s, _ = proj3.shape
    L = SSD_CHUNK
    pad = LANES - SSD_HEADS
    dtb = jnp.pad(dt_bias.astype(F32), (0, pad)).reshape(1, LANES)
    alog = jnp.pad(a_log.astype(F32), (0, pad)).reshape(1, LANES)
    dsk = jnp.repeat(d_skip.astype(F32), SSD_HEAD_DIM).reshape(1, SSD_WIDTH)
    sel = (jnp.arange(LANES)[:, None] == (jnp.arange(SSD_HEADS * LANES)[None, :] // LANES)).astype(BF16)
    const = lambda shape: pl.BlockSpec(shape, lambda b, c: (0,) * len(shape))
    return pl.pallas_call(
        _ssd_kernel,
        grid=(bsz, s // L),
        in_specs=[
            pl.BlockSpec((1, L, SSD_WIDTH), lambda b, c: (b, c, 0)),
            pl.BlockSpec((1, L, SSD_WIDTH), lambda b, c: (b, c, 1)),
            pl.BlockSpec((1, L, 2 * SSD_STATE), lambda b, c: (b, c, 2 * SSD_WIDTH // (2 * SSD_STATE))),
            pl.BlockSpec((1, L, 2 * SSD_STATE), lambda b, c: (b, c, 2 * SSD_WIDTH // (2 * SSD_STATE) + 1)),
            pl.BlockSpec((1, L, LANES), lambda b, c: (b, c, 0)),
            const((SSD_CONV, XBC_WIDTH)), const((1, XBC_WIDTH)), const((1, LANES)), const((1, LANES)),
            const((1, SSD_WIDTH)), const((1, SSD_WIDTH)), const((LANES, SSD_HEADS * LANES)),
        ],
        out_specs=pl.BlockSpec((1, L, SSD_WIDTH), lambda b, c: (b, c, 0)),
        out_shape=jax.ShapeDtypeStruct((bsz, s, SSD_WIDTH), BF16),
        scratch_shapes=[
            pltpu.VMEM((L + 8, XBC_WIDTH), F32),
            pltpu.VMEM((SSD_HEADS // 2, SSD_STATE, LANES), F32),
            pltpu.VMEM((L, SSD_WIDTH), F32),
        ],
        compiler_params=_cparams(("parallel", "arbitrary")),
        name="ssd",
    )(proj3, proj3, proj3, proj3, dt3, conv_w.astype(F32), conv_b.astype(F32).reshape(1, XBC_WIDTH),
      dtb, alog, dsk, norm_g.astype(F32).reshape(1, SSD_WIDTH), sel)


def _qk_norm_kernel(x_ref, g_ref, o_ref):
    x = x_ref[...].astype(F32)
    ra = lax.broadcasted_iota(I32, (LANES, LANES), 0) // DA_HALF
    rb = lax.broadcasted_iota(I32, (LANES, LANES), 1) // DA_HALF
    grp = jnp.where(ra == rb, 1.0, 0.0).astype(BF16)
    hi, mid = _split2(x * x)
    ss = _dot(hi, grp) + _dot(mid, grp)
    o_ref[...] = (x * lax.rsqrt(ss * (1.0 / DA_HALF) + EPS) * g_ref[0]).astype(o_ref.dtype)


def _qk_norm(proj, gains, tm=1024):
    m = proj.shape[0]
    nh = 2 * DA_HEADS
    return pl.pallas_call(
        _qk_norm_kernel,
        grid=(m // tm, nh),
        in_specs=[
            pl.BlockSpec((tm, LANES), lambda i, j: (i, COL_Q + j)),
            pl.BlockSpec((1, 1, LANES), lambda i, j: (j, 0, 0)),
        ],
        out_specs=pl.BlockSpec((tm, LANES), lambda i, j: (i, j)),
        out_shape=jax.ShapeDtypeStruct((m, nh * LANES), BF16),
        compiler_params=_cparams(("parallel", "parallel")),
        name="qk_norm",
    )(proj, gains)


def _da_kernel(q_ref, k_ref, v_ref, lv_ref, sg_ref, o_ref, q2_ref, m_ref, l_ref, acc_ref, *, lam_init, tq, tk):
    qi = pl.program_id(2)
    ki = pl.program_id(3)

    @pl.when(ki == 0)
    def _():
        q = q_ref[0]
        first = lax.broadcasted_iota(I32, q.shape, 1) < DA_HALF
        zero = jnp.zeros_like(q)
        q2_ref[0] = jnp.where(first, q, zero)
        q2_ref[1] = jnp.where(first, zero, q)
        m_ref[...] = jnp.full(m_ref.shape, -jnp.inf, F32)
        l_ref[...] = jnp.zeros_like(l_ref)
        acc_ref[...] = jnp.zeros_like(acc_ref)

    def step(masked):
        k = k_ref[0]
        v = v_ref[0]
        if masked:
            keep = (lax.broadcasted_iota(I32, (tq, tk), 1) <= lax.broadcasted_iota(I32, (tq, tk), 0))
        for j in range(2):
            s = _dot_nt(q2_ref[j], k)
            if masked:
                s = jnp.where(keep, s, -jnp.inf)
            m_prev = m_ref[j]
            m_new = jnp.maximum(m_prev, jnp.max(s, axis=-1, keepdims=True))
            alpha = jnp.exp(m_prev - m_new)
            p = jnp.exp(s - jnp.concatenate([m_new] * (tk // LANES), axis=1))
            l_ref[j] = alpha * l_ref[j] + jnp.sum(p, axis=-1, keepdims=True)
            acc_ref[j] = alpha * acc_ref[j] + _dot(p.astype(BF16), v)
            m_ref[j] = m_new

    @pl.when(ki < qi)
    def _():
        step(False)

    @pl.when(ki == qi)
    def _():
        step(True)
        lv = lv_ref[...]
        lam = (jnp.exp(jnp.sum(lv[0:1] * lv[1:2], axis=-1, keepdims=True))
               - jnp.exp(jnp.sum(lv[2:3] * lv[3:4], axis=-1, keepdims=True)) + lam_init)
        o = acc_ref[0] / l_ref[0] - lam * (acc_ref[1] / l_ref[1])
        o_ref[0] = (_rms(o) * (sg_ref[...] * (1.0 - lam_init))).astype(o_ref.dtype)


def _diff_attn(qk3, proj3, lvec, subln_g, lam_init, tq=512):
    bsz, s, _ = qk3.shape
    tk = tq
    nq = s // tq
    return pl.pallas_call(
        functools.partial(_da_kernel, lam_init=lam_init, tq=tq, tk=tk),
        grid=(bsz, DA_HEADS, nq, nq),
        in_specs=[
            pl.BlockSpec((1, tq, LANES), lambda b, h, qi, ki: (b, qi, h)),
            pl.BlockSpec((1, tk, LANES), lambda b, h, qi, ki: (b, jnp.minimum(ki, qi), DA_HEADS + h)),
            pl.BlockSpec((1, tk, LANES), lambda b, h, qi, ki: (b, jnp.minimum(ki, qi), COL_V + h)),
            pl.BlockSpec((4, DA_HALF), lambda b, h, qi, ki: (0, 0)),
            pl.BlockSpec((1, LANES), lambda b, h, qi, ki: (0, 0)),
        ],
        out_specs=pl.BlockSpec((1, tq, LANES), lambda b, h, qi, ki: (b, qi, h)),
        out_shape=jax.ShapeDtypeStruct((bsz, s, DA_WIDTH), BF16),
        scratch_shapes=[
            pltpu.VMEM((2, tq, LANES), BF16),
            pltpu.VMEM((2, tq, LANES), F32),
            pltpu.VMEM((2, tq, LANES), F32),
            pltpu.VMEM((2, tq, LANES), F32),
        ],
        compiler_params=_cparams(("parallel", "parallel", "parallel", "arbitrary")),
        name="diff_attn",
    )(qk3, qk3, proj3, lvec, subln_g)


def _xattn_kernel(q_ref, k_ref, v_ref, o_ref):
    s = _dot_nt(q_ref[...], k_ref[...])
    p = jnp.exp(s - jnp.max(s, axis=-1, keepdims=True))
    o = _dot(p.astype(BF16), v_ref[...]) / jnp.sum(p, axis=-1, keepdims=True)
    o_ref[...] = o.astype(o_ref.dtype)


def _xattn(qx, kv, seq, mem_len, tm=512):
    m = qx.shape[0]
    hd = XA_HEAD_DIM
    return pl.pallas_call(
        _xattn_kernel,
        grid=(m // tm, XA_HEADS),
        in_specs=[
            pl.BlockSpec((tm, hd), lambda i, j: (i, j)),
            pl.BlockSpec((mem_len, hd), lambda i, j: ((i * tm) // seq, j)),
            pl.BlockSpec((mem_len, hd), lambda i, j: ((i * tm) // seq, XA_HEADS + j)),
        ],
        out_specs=pl.BlockSpec((tm, hd), lambda i, j: (i, j)),
        out_shape=jax.ShapeDtypeStruct((m, XA_HEADS * hd), BF16),
        compiler_params=_cparams(("parallel", "parallel")),
        name="xattn",
    )(qx, kv, kv)


def _router_kernel(x_ref, g_ref, rw_ref, rb_ref, hf_ref, route_ref, gate_ref, cnt_ref, carry_ref, *, tm):
    i = pl.program_id(0)

    @pl.when(i == 0)
    def _():
        carry_ref[...] = jnp.zeros_like(carry_ref)

    hn = _rms(x_ref[...]) * g_ref[...]
    hi, mid = _split2(hn)
    hf_ref[...] = hi.reshape(tm, ROW_SUB, LANES)

    logits = _dot(hi, rw_ref[0]) + (_dot(hi, rw_ref[1]) + _dot(mid, rw_ref[0])) + rb_ref[...]
    lane = lax.broadcasted_iota(I32, (tm, LANES), 1)
    lane_f = lane.astype(F32)
    idxs, vals = [], []
    work = logits
    for _ in range(TOP_K):
        m = jnp.max(work, axis=-1, keepdims=True)
        ik = jnp.min(jnp.where(work == m, lane_f, float(LANES)), axis=-1, keepdims=True)
        idxs.append(ik)
        vals.append(m)
        work = jnp.where(lane_f == ik, -jnp.inf, work)
    es = [jnp.exp(v - vals[0]) for v in vals]
    den = es[0] + es[1] + es[2] + es[3]

    onehot = jnp.zeros((tm, LANES), F32)
    for ik in idxs:
        onehot = onehot + jnp.where(lane_f == ik, 1.0, 0.0)
    r = lax.broadcasted_iota(I32, (tm, tm), 0)
    c = lax.broadcasted_iota(I32, (tm, tm), 1)
    strict = jnp.where(c < r, 1.0, 0.0).astype(BF16)
    before = carry_ref[0:1, :] + _dot(strict, onehot.astype(BF16))
    carry_ref[0:1, :] = carry_ref[0:1, :] + jnp.sum(onehot, axis=0, keepdims=True)
    cnt_ref[...] = jnp.broadcast_to(carry_ref[0:1, :], cnt_ref.shape)

    route = jnp.zeros((tm, LANES), F32)
    gates = jnp.zeros((tm, LANES), F32)
    for kk in range(TOP_K):
        rank = jnp.sum(jnp.where(lane_f == idxs[kk], before, 0.0), axis=-1, keepdims=True)
        route = jnp.where(lane == kk, idxs[kk], route)
        route = jnp.where(lane == TOP_K + kk, rank, route)
        gates = jnp.where(lane == kk, es[kk] / den, gates)
    route_ref[...] = route.astype(I32)
    gate_ref[...] = gates


def _router(h2, g, rw3, rb, tm=512):
    t, d = h2.shape
    return pl.pallas_call(
        functools.partial(_router_kernel, tm=tm),
        grid=(t // tm,),
        in_specs=[
            pl.BlockSpec((tm, d), lambda i: (i, 0)),
            pl.BlockSpec((1, d), lambda i: (0, 0)),
            pl.BlockSpec((3, d, LANES), lambda i: (0, 0, 0)),
            pl.BlockSpec((1, LANES), lambda i: (0, 0)),
        ],
        out_specs=[
            pl.BlockSpec((tm, ROW_SUB, LANES), lambda i: (i, 0, 0)),
            pl.BlockSpec((tm, LANES), lambda i: (i, 0)),
            pl.BlockSpec((tm, LANES), lambda i: (i, 0)),
            pl.BlockSpec((8, LANES), lambda i: (0, 0)),
        ],
        out_shape=[
            jax.ShapeDtypeStruct((t, ROW_SUB, LANES), BF16),
            jax.ShapeDtypeStruct((t, LANES), I32),
            jax.ShapeDtypeStruct((t, LANES), F32),
            jax.ShapeDtypeStruct((8, LANES), F32),
        ],
        scratch_shapes=[pltpu.VMEM((8, LANES), F32)],
        compiler_params=_cparams(("arbitrary",)),
        name="router",
    )(h2, g, rw3, rb)


def _dispatch_kernel(dest_ref, nr_ref, hf_ref, xs_ref, zero_ref, sem, zsem, *, tt, n_blocks):
    base = pl.program_id(0) * tt

    @pl.when(pl.program_id(0) == 0)
    def _():
        zero_ref[...] = jnp.zeros_like(zero_ref)

        def fill(start):
            def body(b, carry):
                pad = MOE_TM - nr_ref[b]
                off = b * MOE_TM + nr_ref[b]
                size = MOE_TM
                while size >= 1:
                    hit = (pad & size) != 0
                    cp = pltpu.make_async_copy(zero_ref.at[pl.ds(0, size)], xs_ref.at[pl.ds(off, size)], zsem)

                    @pl.when(hit)
                    def _(cp=cp):
                        cp.start() if start else cp.wait()

                    off = off + jnp.where(hit, size, 0)
                    size //= 2
                return carry
            lax.fori_loop(0, n_blocks, body, 0)

        fill(True)
        fill(False)

    def copy(t, kk):
        return pltpu.make_async_copy(hf_ref.at[base + t], xs_ref.at[dest_ref[0, 0, t * TOP_K + kk]], sem)

    def issue(t, carry):
        for kk in range(TOP_K):
            copy(t, kk).start()
        return carry

    def drain(t, carry):
        for kk in range(TOP_K):
            copy(t, kk).wait()
        return carry

    lax.fori_loop(0, tt, issue, 0)
    lax.fori_loop(0, tt, drain, 0)


def _dispatch(dest2, nr, hf3, n_blocks, tt):
    t = hf3.shape[0]
    return pl.pallas_call(
        functools.partial(_dispatch_kernel, tt=tt, n_blocks=n_blocks),
        grid=(t // tt,),
        in_specs=[
            pl.BlockSpec((1, 1, tt * TOP_K), lambda i: (i, 0, 0), memory_space=pltpu.SMEM),
            pl.BlockSpec(memory_space=pltpu.SMEM),
            pl.BlockSpec(memory_space=pl.ANY),
        ],
        out_specs=pl.BlockSpec(memory_space=pl.ANY),
        out_shape=jax.ShapeDtypeStruct((n_blocks * MOE_TM, ROW_SUB, LANES), BF16),
        scratch_shapes=[pltpu.VMEM((MOE_TM, ROW_SUB, LANES), BF16), pltpu.SemaphoreType.DMA(()),
                        pltpu.SemaphoreType.DMA(())],
        compiler_params=_cparams(("arbitrary",)),
        name="dispatch",
    )(dest2, nr, hf3)


def _moe_up_kernel(be_ref, bx_ref, nr_ref, first_ref, x_ref, wg_ref, wu_ref, bg_ref, bu_ref, o_ref,
                   wgb_ref, wub_ref):
    i = pl.program_id(1)
    nrows = nr_ref[i]

    @pl.when(first_ref[i] == 1)
    def _():
        wgb_ref[...] = wg_ref[0].astype(BF16)
        wub_ref[...] = wu_ref[0].astype(BF16)

    for h in range(MOE_TM // MOE_SUB):
        rows = slice(h * MOE_SUB, (h + 1) * MOE_SUB)

        @pl.when(nrows > h * MOE_SUB)
        def _(rows=rows):
            xb = x_ref[rows].reshape(MOE_SUB, D_MODEL)
            g = jnp.minimum(_dot(xb, wgb_ref[...]) + bg_ref[0], SWIGLU_LIMIT)
            u = jnp.clip(_dot(xb, wub_ref[...]) + bu_ref[0], -SWIGLU_LIMIT, SWIGLU_LIMIT)
            act = g * jax.nn.sigmoid(SWIGLU_ALPHA * g) * (u + 1.0)
            o_ref[rows, :] = act.astype(o_ref.dtype)

        @pl.when(nrows <= h * MOE_SUB)
        def _(rows=rows):
            o_ref[rows, :] = jnp.zeros((MOE_SUB, MOE_TF), o_ref.dtype)


def _moe_up(meta, xs3, w1, b1, n_blocks):
    be, bx, nr, first = meta
    n_slots = xs3.shape[0]
    nf = D_FF // MOE_TF
    grid_spec = pltpu.PrefetchScalarGridSpec(
        num_scalar_prefetch=4,
        grid=(nf, n_blocks),
        in_specs=[
            pl.BlockSpec((MOE_TM, ROW_SUB, LANES), lambda f, i, be, bx, nr, fi: (bx[i], 0, 0)),
            pl.BlockSpec((1, D_MODEL, MOE_TF), lambda f, i, be, bx, nr, fi: (be[i], 0, f)),
            pl.BlockSpec((1, D_MODEL, MOE_TF), lambda f, i, be, bx, nr, fi: (be[i], 0, nf + f)),
            pl.BlockSpec((1, 1, MOE_TF), lambda f, i, be, bx, nr, fi: (be[i], 0, f)),
            pl.BlockSpec((1, 1, MOE_TF), lambda f, i, be, bx, nr, fi: (be[i], 0, nf + f)),
        ],
        out_specs=pl.BlockSpec((MOE_TM, MOE_TF), lambda f, i, be, bx, nr, fi: (i, f)),
        scratch_shapes=[pltpu.VMEM((D_MODEL, MOE_TF), BF16), pltpu.VMEM((D_MODEL, MOE_TF), BF16)],
    )
    return pl.pallas_call(
        _moe_up_kernel,
        grid_spec=grid_spec,
        out_shape=jax.ShapeDtypeStruct((n_slots, D_FF), BF16),
        compiler_params=_cparams(("arbitrary", "arbitrary")),
        name="moe_up",
    )(be, bx, nr, first, xs3, w1, w1, b1, b1)


def _moe_down_kernel(be_ref, bx_ref, nr_ref, first_ref, a_ref, w_ref, b_ref, o_ref, wb_ref):
    i = pl.program_id(0)
    nrows = nr_ref[i]

    @pl.when(first_ref[i] == 1)
    def _():
        wb_ref[...] = w_ref[0].astype(BF16)

    for h in range(MOE_TM // MOE_SUB):
        rows = slice(h * MOE_SUB, (h + 1) * MOE_SUB)

        @pl.when(nrows > h * MOE_SUB)
        def _(rows=rows):
            y = _dot(a_ref[rows, :], wb_ref[...]) + b_ref[0]
            o_ref[rows] = y.astype(o_ref.dtype).reshape(MOE_SUB, ROW_SUB, LANES)

        @pl.when(nrows <= h * MOE_SUB)
        def _(rows=rows):
            o_ref[rows] = jnp.zeros((MOE_SUB, ROW_SUB, LANES), o_ref.dtype)


def _moe_down(meta, act, w2, b2, n_blocks):
    be, bx, nr, first = meta
    n_slots = act.shape[0]
    grid_spec = pltpu.PrefetchScalarGridSpec(
        num_scalar_prefetch=4,
        grid=(n_blocks,),
        in_specs=[
            pl.BlockSpec((MOE_TM, D_FF), lambda i, be, bx, nr, fi: (bx[i], 0)),
            pl.BlockSpec((1, D_FF, D_MODEL), lambda i, be, bx, nr, fi: (be[i], 0, 0)),
            pl.BlockSpec((1, 1, D_MODEL), lambda i, be, bx, nr, fi: (be[i], 0, 0)),
        ],
        out_specs=pl.BlockSpec((MOE_TM, ROW_SUB, LANES), lambda i, be, bx, nr, fi: (i, 0, 0)),
        scratch_shapes=[pltpu.VMEM((D_FF, D_MODEL), BF16)],
    )
    return pl.pallas_call(
        _moe_down_kernel,
        grid_spec=grid_spec,
        out_shape=jax.ShapeDtypeStruct((n_slots, ROW_SUB, LANES), BF16),
        compiler_params=_cparams(("arbitrary",)),
        name="moe_down",
    )(be, bx, nr, first, act, w2, b2)


def _combine_kernel(dest_ref, y_ref, gate_ref, h_ref, o_ref, buf_ref, sem, *, tc):
    def copy(t, kk):
        return pltpu.make_async_copy(y_ref.at[dest_ref[0, 0, t * TOP_K + kk]], buf_ref.at[kk * tc + t], sem)

    def issue(t, carry):
        for kk in range(TOP_K):
            copy(t, kk).start()
        return carry

    def drain(t, carry):
        for kk in range(TOP_K):
            copy(t, kk).wait()
        return carry

    lax.fori_loop(0, tc, issue, 0)
    lax.fori_loop(0, tc, drain, 0)

    gates = gate_ref[...]
    acc = h_ref[...]
    for kk in range(TOP_K):
        yk = buf_ref[kk * tc:(kk + 1) * tc].reshape(tc, ROW_SUB * LANES).astype(F32)
        acc = acc + gates[:, kk:kk + 1] * yk
    o_ref[...] = acc


def _combine(dest2, y3, gates, h2, tc):
    t, d = h2.shape
    return pl.pallas_call(
        functools.partial(_combine_kernel, tc=tc),
        grid=(t // tc,),
        in_specs=[
            pl.BlockSpec((1, 1, tc * TOP_K), lambda i: (i, 0, 0), memory_space=pltpu.SMEM),
            pl.BlockSpec(memory_space=pl.ANY),
            pl.BlockSpec((tc, LANES), lambda i: (i, 0)),
            pl.BlockSpec((tc, d), lambda i: (i, 0)),
        ],
        out_specs=pl.BlockSpec((tc, d), lambda i: (i, 0)),
        out_shape=jax.ShapeDtypeStruct((t, d), F32),
        scratch_shapes=[pltpu.VMEM((TOP_K * tc, ROW_SUB, LANES), BF16), pltpu.SemaphoreType.DMA(())],
        compiler_params=_cparams(("arbitrary",)),
        name="combine",
    )(dest2, y3, gates, h2)


def _moe_plan(counts, route, n_blocks):
    idx = route[:, :TOP_K]
    rank = route[:, TOP_K:2 * TOP_K]
    nblk = (counts + MOE_TM - 1) // MOE_TM
    blk_end = jnp.cumsum(nblk)
    blk_start = blk_end - nblk
    dest = (blk_start * MOE_TM)[idx] + rank
    total = blk_end[-1]
    ids = jnp.arange(n_blocks, dtype=I32)
    live = ids < total
    src = jnp.minimum(ids, total - 1)
    be = jnp.minimum(jnp.sum(blk_end[None, :] <= src[:, None], axis=1), N_EXPERTS - 1).astype(I32)
    nr = jnp.where(live, jnp.clip(counts[be] - (ids - blk_start[be]) * MOE_TM, 0, MOE_TM), 0).astype(I32)
    first = jnp.logical_and(live, ids == blk_start[be]).astype(I32)
    return dest.astype(I32), (be, src.astype(I32), nr, first)


def _layer(h, mem, lam_init, norm_mix_g, w_in, conv_w, conv_b, dt_bias, a_log, d_skip, ssd_norm_g,
           da_q_norm_g, da_k_norm_g, lq1, lk1, lq2, lk2, da_subln_g, w_out, norm_xa_g, norm_mem_g,
           xa_wq, xa_wkv, xa_q_norm_g, xa_k_norm_g, xa_wo, norm_ffn_g, router_w, router_b,
           moe_w1, moe_b1, moe_w2, moe_b2):
    bsz, seq, d = h.shape
    t = bsz * seq
    mem_len = mem.shape[1]
    row = lambda v: v.astype(F32).reshape(1, -1)
    x2 = h.reshape(t, d)

    o1 = SSD_WIDTH + XBC_WIDTH
    o2 = o1 + SSD_HEADS
    w_main = jnp.concatenate([w_in[:, :o1], w_in[:, o2:]], axis=1).astype(BF16)
    w_dt = _split3_host(jnp.pad(w_in[:, o1:o2], ((0, 0), (0, LANES - SSD_HEADS))))
    proj, dt_raw = _in_proj(x2, row(norm_mix_g), w_main, w_dt)
    proj3 = proj.reshape(bsz, seq, PROJ_COLS)
    y_ssd = _ssd(proj3, dt_raw.reshape(bsz, seq, LANES), conv_w, conv_b, dt_bias, a_log, d_skip, ssd_norm_g)

    gq = jnp.tile(da_q_norm_g.astype(F32), 2) * (DA_HALF ** -0.5)
    gk = jnp.tile(da_k_norm_g.astype(F32), 2)
    gains = jnp.concatenate([jnp.tile(gq[None], (DA_HEADS, 1)), jnp.tile(gk[None], (DA_HEADS, 1))])
    qk = _qk_norm(proj, gains.reshape(2 * DA_HEADS, 1, LANES))
    lvec = jnp.stack([lq1, lk1, lq2, lk2]).astype(F32)
    y_da = _diff_attn(qk.reshape(bsz, seq, 2 * DA_WIDTH), proj3, lvec, row(da_subln_g), lam_init)

    h1 = _mm_res([y_ssd.reshape(t, SSD_WIDTH), y_da.reshape(t, DA_WIDTH)], w_out.astype(BF16), x2)

    eq = jnp.tile(xa_q_norm_g.astype(F32) * (XA_HEAD_DIM ** -0.5), XA_HEADS).reshape(1, -1)
    qx = _norm_proj(h1, row(norm_xa_g), xa_wq.astype(BF16), eq[:, :XA_HEAD_DIM], XA_HEADS, 1024, XA_HEAD_DIM)
    kv = _norm_proj(mem.reshape(bsz * mem_len, d), row(norm_mem_g), xa_wkv.astype(BF16),
                    row(xa_k_norm_g), XA_HEADS, bsz * mem_len, XA_HEAD_DIM)
    ox = _xattn(qx, kv, seq, mem_len)
    h2 = _mm_res([ox], xa_wo.astype(BF16), h1)

    rw3 = _split3_host(jnp.pad(router_w, ((0, 0), (0, LANES - N_EXPERTS))))
    rb = jnp.pad(router_b.astype(F32), (0, LANES - N_EXPERTS), constant_values=-jnp.inf).reshape(1, LANES)
    hf, route, gates, cnt = _router(h2, row(norm_ffn_g), rw3, rb)
    n_blocks = (t * TOP_K) // MOE_TM + N_EXPERTS
    n_slots = n_blocks * MOE_TM
    dest, meta = _moe_plan(cnt[0, :N_EXPERTS].astype(I32), route, n_blocks)
    tt = 512
    xs = _dispatch(dest.reshape(t // tt, 1, tt * TOP_K), meta[2], hf, n_blocks, tt)
    act = _moe_up(meta, xs, moe_w1,
                  moe_b1.reshape(N_EXPERTS, 1, 2 * D_FF), n_blocks)
    y = _moe_down(meta, act, moe_w2, moe_b2.reshape(N_EXPERTS, 1, D_MODEL), n_blocks)
    tc = 256
    out = _combine(dest.reshape(t // tc, 1, tc * TOP_K), y, gates, h2, tc)
    return out.reshape(bsz, seq, d)


def kernel(x, mem, norm_mix_g, w_in, conv_w, conv_b, dt_bias, a_log, d_skip, ssd_norm_g, da_q_norm_g,
           da_k_norm_g, lambda_q1, lambda_k1, lambda_q2, lambda_k2, da_subln_g, w_out, norm_xa_g,
           norm_mem_g, xa_wq, xa_wkv, xa_q_norm_g, xa_k_norm_g, xa_wo, norm_ffn_g, router_w, router_b,
           moe_w1, moe_b1, moe_w2, moe_b2):
    h = x
    for layer in range(norm_mix_g.shape[0]):
        lam_init = 0.8 - 0.6 * math.exp(-0.3 * layer)
        h = _layer(h, mem, lam_init, norm_mix_g[layer], w_in[layer], conv_w[layer], conv_b[layer],
                   dt_bias[layer], a_log[layer], d_skip[layer], ssd_norm_g[layer], da_q_norm_g[layer],
                   da_k_norm_g[layer], lambda_q1[layer], lambda_k1[layer], lambda_q2[layer],
                   lambda_k2[layer], da_subln_g[layer], w_out[layer], norm_xa_g[layer], norm_mem_g[layer],
                   xa_wq[layer], xa_wkv[layer], xa_q_norm_g[layer], xa_k_norm_g[layer], xa_wo[layer],
                   norm_ffn_g[layer], router_w[layer], router_b[layer], moe_w1[layer], moe_b1[layer],
                   moe_w2[layer], moe_b2[layer])
    return h
```

```python
import functools
import math

import jax
import jax.numpy as jnp
from jax import lax
from jax.experimental import pallas as pl
from jax.experimental.pallas import tpu as pltpu

F32 = jnp.float32
BF16 = jnp.bfloat16
I32 = jnp.int32

D_MODEL = 2048
SSD_WIDTH = 1024
SSD_HEAD_DIM = 64
SSD_HEADS = 16
SSD_GROUPS = 2
SSD_STATE = 128
SSD_CONV = 4
SSD_CHUNK = 128
XBC_WIDTH = SSD_WIDTH + 2 * SSD_GROUPS * SSD_STATE
DA_WIDTH = 1024
DA_HEAD_DIM = 128
DA_HALF = 64
DA_HEADS = 8
XA_HEADS = 4
XA_HEAD_DIM = D_MODEL // XA_HEADS
N_EXPERTS = 32
TOP_K = 4
D_FF = 2048
SWIGLU_LIMIT = 7.0
SWIGLU_ALPHA = 1.702
EPS = 1e-6

LANES = 128
ROW_SUB = D_MODEL // LANES
VMEM_LIMIT = 56 * 1024 * 1024

PROJ_COLS = SSD_WIDTH + XBC_WIDTH + 3 * DA_WIDTH
COL_Q = (SSD_WIDTH + XBC_WIDTH) // LANES
COL_V = COL_Q + 2 * DA_HEADS

MOE_TM = 512
MOE_SUB = 256
MOE_TF = 1024


def _dot(a, b):
    return jnp.dot(a, b, preferred_element_type=F32)


def _dot_nt(a, b):
    return lax.dot_general(a, b, (((1,), (1,)), ((), ())), preferred_element_type=F32)


def _dot_tn(a, b):
    return lax.dot_general(a, b, (((0,), (0,)), ((), ())), preferred_element_type=F32)


def _rms(x):
    return x * lax.rsqrt(jnp.mean(x * x, axis=-1, keepdims=True) + EPS)


def _split2(x):
    hi = x.astype(BF16)
    mid = (x - hi.astype(F32)).astype(BF16)
    return hi, mid


def _split3(x):
    hi = x.astype(BF16)
    r = x - hi.astype(F32)
    mid = r.astype(BF16)
    lo = (r - mid.astype(F32)).astype(BF16)
    return hi, mid, lo


def _split3_host(w):
    hi, mid, lo = _split3(w.astype(F32))
    return jnp.stack([hi, mid, lo])


def _silu(x):
    return x * jax.nn.sigmoid(x)


def _cparams(sem):
    return pltpu.CompilerParams(dimension_semantics=sem, vmem_limit_bytes=VMEM_LIMIT)


def _in_proj_kernel(x_ref, g_ref, w_ref, wdt_ref, o_ref, dt_ref, xn_ref):
    @pl.when(pl.program_id(1) == 0)
    def _():
        hn = _rms(x_ref[...]) * g_ref[...]
        hi, mid = _split2(hn)
        xn_ref[...] = hi
        dt_ref[...] = _dot(hi, wdt_ref[0]) + (_dot(hi, wdt_ref[1]) + _dot(mid, wdt_ref[0]))

    o_ref[...] = _dot(xn_ref[...], w_ref[...]).astype(o_ref.dtype)


def _in_proj(x2, g, w, wdt3, tm=1024, tn=512):
    m, k = x2.shape
    n = w.shape[1]
    return pl.pallas_call(
        _in_proj_kernel,
        grid=(m // tm, n // tn),
        in_specs=[
            pl.BlockSpec((tm, k), lambda i, j: (i, 0)),
            pl.BlockSpec((1, k), lambda i, j: (0, 0)),
            pl.BlockSpec((k, tn), lambda i, j: (0, j)),
            pl.BlockSpec((3, k, LANES), lambda i, j: (0, 0, 0)),
        ],
        out_specs=[
            pl.BlockSpec((tm, tn), lambda i, j: (i, j)),
            pl.BlockSpec((tm, LANES), lambda i, j: (i, 0)),
        ],
        out_shape=[jax.ShapeDtypeStruct((m, n), BF16), jax.ShapeDtypeStruct((m, LANES), F32)],
        scratch_shapes=[pltpu.VMEM((tm, k), BF16)],
        compiler_params=_cparams(("parallel", "arbitrary")),
        name="in_proj",
    )(x2, g, w, wdt3)


def _norm_proj_kernel(x_ref, g_ref, w_ref, eg_ref, o_ref, xn_ref, *, epi_tiles):
    j = pl.program_id(1)

    @pl.when(j == 0)
    def _():
        xn_ref[...] = (_rms(x_ref[...]) * g_ref[...]).astype(BF16)

    acc = _dot(xn_ref[...], w_ref[...])

    @pl.when(j < epi_tiles)
    def _():
        o_ref[...] = (_rms(acc) * eg_ref[...]).astype(o_ref.dtype)

    @pl.when(j >= epi_tiles)
    def _():
        o_ref[...] = acc.astype(o_ref.dtype)


def _norm_proj(x2, g, w, eg, epi_tiles, tm, tn):
    m, k = x2.shape
    n = w.shape[1]
    return pl.pallas_call(
        functools.partial(_norm_proj_kernel, epi_tiles=epi_tiles),
        grid=(m // tm, n // tn),
        in_specs=[
            pl.BlockSpec((tm, k), lambda i, j: (i, 0)),
            pl.BlockSpec((1, k), lambda i, j: (0, 0)),
            pl.BlockSpec((k, tn), lambda i, j: (0, j)),
            pl.BlockSpec((1, tn), lambda i, j: (0, 0)),
        ],
        out_specs=pl.BlockSpec((tm, tn), lambda i, j: (i, j)),
        out_shape=jax.ShapeDtypeStruct((m, n), BF16),
        scratch_shapes=[pltpu.VMEM((tm, k), BF16)],
        compiler_params=_cparams(("parallel", "arbitrary")),
        name="norm_proj",
    )(x2, g, w, eg)


def _mm_res_kernel(*refs, n_a):
    a_refs, w_refs = refs[:n_a], refs[n_a:2 * n_a]
    res_ref, o_ref = refs[2 * n_a], refs[2 * n_a + 1]
    acc = res_ref[...]
    for a, w in zip(a_refs, w_refs):
        acc = acc + _dot(a[...], w[...])
    o_ref[...] = acc


def _mm_res(a_list, w, res, tm=1024, tn=512):
    m, n = res.shape
    n_a = len(a_list)
    in_specs = []
    for a in a_list:
        in_specs.append(pl.BlockSpec((tm, a.shape[1]), lambda i, j: (i, 0)))
    for idx, a in enumerate(a_list):
        in_specs.append(pl.BlockSpec((a.shape[1], tn), lambda i, j, idx=idx: (idx, j)))
    in_specs.append(pl.BlockSpec((tm, tn), lambda i, j: (i, j)))
    return pl.pallas_call(
        functools.partial(_mm_res_kernel, n_a=n_a),
        grid=(m // tm, n // tn),
        in_specs=in_specs,
        out_specs=pl.BlockSpec((tm, tn), lambda i, j: (i, j)),
        out_shape=jax.ShapeDtypeStruct((m, n), F32),
        compiler_params=_cparams(("parallel", "arbitrary")),
        name="mm_res",
    )(*a_list, *([w] * n_a), res)


def _ssd_kernel(z_ref, xs_ref, b_ref, c_ref, dtr_ref, cw_ref, cb_ref, dtb_ref, alog_ref,
                dsk_ref, ng_ref, sel_ref, o_ref, ext_ref, ht_ref, y_ref):
    L = SSD_CHUNK
    n_pairs = SSD_HEADS // 2
    pairs_per_group = n_pairs // SSD_GROUPS
    c_off = SSD_WIDTH + SSD_GROUPS * SSD_STATE

    @pl.when(pl.program_id(1) == 0)
    def _():
        ht_ref[...] = jnp.zeros_like(ht_ref)
        ext_ref[0:8, :] = jnp.zeros((8, XBC_WIDTH), F32)

    ext_ref[8:8 + L, 0:SSD_WIDTH] = xs_ref[0].astype(F32)
    ext_ref[8:8 + L, SSD_WIDTH:c_off] = b_ref[0].astype(F32)
    ext_ref[8:8 + L, c_off:XBC_WIDTH] = c_ref[0].astype(F32)
    acc = jnp.broadcast_to(cb_ref[...], (L, XBC_WIDTH))
    for k in range(SSD_CONV):
        acc = acc + cw_ref[k:k + 1, :] * ext_ref[pl.ds(8 - (SSD_CONV - 1) + k, L), :]
    ext_ref[0:8, :] = ext_ref[L:L + 8, :]
    xbc = _silu(acc)

    dt = jax.nn.softplus(dtr_ref[0] + dtb_ref[...])
    a = -jnp.exp(alog_ref[...])
    adt = dt * a
    row = lax.broadcasted_iota(I32, (L, L), 0)
    col = lax.broadcasted_iota(I32, (L, L), 1)
    causal = col <= row
    tril = jnp.where(causal, 1.0, 0.0).astype(BF16)
    a_hi, a_mid, a_lo = _split3(adt)
    acum = _dot(tril, a_hi) + (_dot(tril, a_mid) + _dot(tril, a_lo))
    acum_t = acum.T
    pieces = [jnp.concatenate([p, q], axis=0) for p, q in zip(_split3(acum), _split3(dt))]

    lane = lax.broadcasted_iota(I32, (L, LANES), 1)
    first = lane < SSD_HEAD_DIM

    cbs = []
    for g in range(SSD_GROUPS):
        bg = xbc[:, SSD_WIDTH + g * SSD_STATE:SSD_WIDTH + (g + 1) * SSD_STATE].astype(BF16)
        cg = xbc[:, c_off + g * SSD_STATE:c_off + (g + 1) * SSD_STATE].astype(BF16)
        cbs.append((bg, cg, _dot_nt(cg, bg)))

    for p in range(n_pairs):
        bg, cg, cb = cbs[p // pairs_per_group]
        sel = sel_ref[:, 2 * LANES * p:2 * LANES * (p + 1)]
        fb = _dot(pieces[0], sel) + (_dot(pieces[1], sel) + _dot(pieces[2], sel))
        ms = []
        for i in range(2):
            h = 2 * p + i
            seg = fb[:L, i * LANES:(i + 1) * LANES] - acum_t[h:h + 1, :]
            dec = jnp.exp(jnp.where(causal, seg, -jnp.inf))
            ms.append((cb * dec).astype(BF16))
        ac = jnp.where(first, fb[:L, :LANES], fb[:L, LANES:])
        dtp = jnp.where(first, fb[L:, :LANES], fb[L:, LANES:])
        xp = xbc[:, p * LANES:(p + 1) * LANES]
        xdt = xp * dtp
        y = _dot(ms[0], jnp.where(first, xdt, 0.0).astype(BF16))
        y = y + _dot(ms[1], jnp.where(first, 0.0, xdt).astype(BF16))
        ht = ht_ref[p]
        y = y + _dot(cg, ht.astype(BF16)) * jnp.exp(ac)
        a_last = ac[L - 1:L, :]
        ht_ref[p] = ht * jnp.exp(a_last) + _dot_tn(bg, (xdt * jnp.exp(a_last - ac)).astype(BF16))
        y_ref[:, p * LANES:(p + 1) * LANES] = y + xp * dsk_ref[:, p * LANES:(p + 1) * LANES]

    yg = y_ref[...] * _silu(z_ref[0].astype(F32))
    gw = SSD_WIDTH // SSD_GROUPS
    for g in range(SSD_GROUPS):
        blk = _rms(yg[:, g * gw:(g + 1) * gw]) * ng_ref[:, g * gw:(g + 1) * gw]
        o_ref[0, :, g * gw:(g + 1) * gw] = blk.astype(o_ref.dtype)


def _ssd(proj3, dt3, conv_w, conv_b, dt_bias, a_log, d_skip, norm_g):
    bsz, s, _ = proj3.shape
    L = SSD_CHUNK
    pad = LANES - SSD_HEADS
    dtb = jnp.pad(dt_bias.astype(F32), (0, pad)).reshape(1, LANES)
    alog = jnp.pad(a_log.astype(F32), (0, pad)).reshape(1, LANES)
    dsk = jnp.repeat(d_skip.astype(F32), SSD_HEAD_DIM).reshape(1, SSD_WIDTH)
    sel = (jnp.arange(LANES)[:, None] == (jnp.arange(SSD_HEADS * LANES)[None, :] // LANES)).astype(BF16)
    const = lambda shape: pl.BlockSpec(shape, lambda b, c: (0,) * len(shape))
    return pl.pallas_call(
        _ssd_kernel,
        grid=(bsz, s // L),
        in_specs=[
            pl.BlockSpec((1, L, SSD_WIDTH), lambda b, c: (b, c, 0)),
            pl.BlockSpec((1, L, SSD_WIDTH), lambda b, c: (b, c, 1)),
            pl.BlockSpec((1, L, 2 * SSD_STATE), lambda b, c: (b, c, 2 * SSD_WIDTH // (2 * SSD_STATE))),
            pl.BlockSpec((1, L, 2 * SSD_STATE), lambda b, c: (b, c, 2 * SSD_WIDTH // (2 * SSD_STATE) + 1)),
            pl.BlockSpec((1, L, LANES), lambda b, c: (b, c, 0)),
            const((SSD_CONV, XBC_WIDTH)), const((1, XBC_WIDTH)), const((1, LANES)), const((1, LANES)),
            const((1, SSD_WIDTH)), const((1, SSD_WIDTH)), const((LANES, SSD_HEADS * LANES)),
        ],
        out_specs=pl.BlockSpec((1, L, SSD_WIDTH), lambda b, c: (b, c, 0)),
        out_shape=jax.ShapeDtypeStruct((bsz, s, SSD_WIDTH), BF16),
        scratch_shapes=[
            pltpu.VMEM((L + 8, XBC_WIDTH), F32),
            pltpu.VMEM((SSD_HEADS // 2, SSD_STATE, LANES), F32),
            pltpu.VMEM((L, SSD_WIDTH), F32),
        ],
        compiler_params=_cparams(("parallel", "arbitrary")),
        name="ssd",
    )(proj3, proj3, proj3, proj3, dt3, conv_w.astype(F32), conv_b.astype(F32).reshape(1, XBC_WIDTH),
      dtb, alog, dsk, norm_g.astype(F32).reshape(1, SSD_WIDTH), sel)


def _qk_norm_kernel(x_ref, g_ref, o_ref):
    x = x_ref[...].astype(F32)
    ra = lax.broadcasted_iota(I32, (LANES, LANES), 0) // DA_HALF
    rb = lax.broadcasted_iota(I32, (LANES, LANES), 1) // DA_HALF
    grp = jnp.where(ra == rb, 1.0, 0.0).astype(BF16)
    hi, mid = _split2(x * x)
    ss = _dot(hi, grp) + _dot(mid, grp)
    o_ref[...] = (x * lax.rsqrt(ss * (1.0 / DA_HALF) + EPS) * g_ref[0]).astype(o_ref.dtype)


def _qk_norm(proj, gains, tm=1024):
    m = proj.shape[0]
    nh = 2 * DA_HEADS
    return pl.pallas_call(
        _qk_norm_kernel,
        grid=(m // tm, nh),
        in_specs=[
            pl.BlockSpec((tm, LANES), lambda i, j: (i, COL_Q + j)),
            pl.BlockSpec((1, 1, LANES), lambda i, j: (j, 0, 0)),
        ],
        out_specs=pl.BlockSpec((tm, LANES), lambda i, j: (i, j)),
        out_shape=jax.ShapeDtypeStruct((m, nh * LANES), BF16),
        compiler_params=_cparams(("parallel", "parallel")),
        name="qk_norm",
    )(proj, gains)


def _da_kernel(q_ref, k_ref, v_ref, lv_ref, sg_ref, o_ref, q2_ref, m_ref, l_ref, acc_ref, *, lam_init, tq, tk):
    qi = pl.program_id(2)
    ki = pl.program_id(3)

    @pl.when(ki == 0)
    def _():
        q = q_ref[0]
        first = lax.broadcasted_iota(I32, q.shape, 1) < DA_HALF
        zero = jnp.zeros_like(q)
        q2_ref[0] = jnp.where(first, q, zero)
        q2_ref[1] = jnp.where(first, zero, q)
        m_ref[...] = jnp.full(m_ref.shape, -jnp.inf, F32)
        l_ref[...] = jnp.zeros_like(l_ref)
        acc_ref[...] = jnp.zeros_like(acc_ref)

    def step(masked):
        k = k_ref[0]
        v = v_ref[0]
        if masked:
            keep = (lax.broadcasted_iota(I32, (tq, tk), 1) <= lax.broadcasted_iota(I32, (tq, tk), 0))
        for j in range(2):
            s = _dot_nt(q2_ref[j], k)
            if masked:
                s = jnp.where(keep, s, -jnp.inf)
            m_prev = m_ref[j]
            m_new = jnp.maximum(m_prev, jnp.max(s, axis=-1, keepdims=True))
            alpha = jnp.exp(m_prev - m_new)
            p = jnp.exp(s - jnp.concatenate([m_new] * (tk // LANES), axis=1))
            l_ref[j] = alpha * l_ref[j] + jnp.sum(p, axis=-1, keepdims=True)
            acc_ref[j] = alpha * acc_ref[j] + _dot(p.astype(BF16), v)
            m_ref[j] = m_new

    @pl.when(ki < qi)
    def _():
        step(False)

    @pl.when(ki == qi)
    def _():
        step(True)
        lv = lv_ref[...]
        lam = (jnp.exp(jnp.sum(lv[0:1] * lv[1:2], axis=-1, keepdims=True))
               - jnp.exp(jnp.sum(lv[2:3] * lv[3:4], axis=-1, keepdims=True)) + lam_init)
        o = acc_ref[0] / l_ref[0] - lam * (acc_ref[1] / l_ref[1])
        o_ref[0] = (_rms(o) * (sg_ref[...] * (1.0 - lam_init))).astype(o_ref.dtype)


def _diff_attn(qk3, proj3, lvec, subln_g, lam_init, tq=512):
    bsz, s, _ = qk3.shape
    tk = tq
    nq = s // tq
    return pl.pallas_call(
        functools.partial(_da_kernel, lam_init=lam_init, tq=tq, tk=tk),
        grid=(bsz, DA_HEADS, nq, nq),
        in_specs=[
            pl.BlockSpec((1, tq, LANES), lambda b, h, qi, ki: (b, qi, h)),
            pl.BlockSpec((1, tk, LANES), lambda b, h, qi, ki: (b, jnp.minimum(ki, qi), DA_HEADS + h)),
            pl.BlockSpec((1, tk, LANES), lambda b, h, qi, ki: (b, jnp.minimum(ki, qi), COL_V + h)),
            pl.BlockSpec((4, DA_HALF), lambda b, h, qi, ki: (0, 0)),
            pl.BlockSpec((1, LANES), lambda b, h, qi, ki: (0, 0)),
        ],
        out_specs=pl.BlockSpec((1, tq, LANES), lambda b, h, qi, ki: (b, qi, h)),
        out_shape=jax.ShapeDtypeStruct((bsz, s, DA_WIDTH), BF16),
        scratch_shapes=[
            pltpu.VMEM((2, tq, LANES), BF16),
            pltpu.VMEM((2, tq, LANES), F32),
            pltpu.VMEM((2, tq, LANES), F32),
            pltpu.VMEM((2, tq, LANES), F32),
        ],
        compiler_params=_cparams(("parallel", "parallel", "parallel", "arbitrary")),
        name="diff_attn",
    )(qk3, qk3, proj3, lvec, subln_g)


def _xattn_kernel(q_ref, k_ref, v_ref, o_ref):
    s = _dot_nt(q_ref[...], k_ref[...])
    p = jnp.exp(s - jnp.max(s, axis=-1, keepdims=True))
    o = _dot(p.astype(BF16), v_ref[...]) / jnp.sum(p, axis=-1, keepdims=True)
    o_ref[...] = o.astype(o_ref.dtype)


def _xattn(qx, kv, seq, mem_len, tm=512):
    m = qx.shape[0]
    hd = XA_HEAD_DIM
    return pl.pallas_call(
        _xattn_kernel,
        grid=(m // tm, XA_HEADS),
        in_specs=[
            pl.BlockSpec((tm, hd), lambda i, j: (i, j)),
            pl.BlockSpec((mem_len, hd), lambda i, j: ((i * tm) // seq, j)),
            pl.BlockSpec((mem_len, hd), lambda i, j: ((i * tm) // seq, XA_HEADS + j)),
        ],
        out_specs=pl.BlockSpec((tm, hd), lambda i, j: (i, j)),
        out_shape=jax.ShapeDtypeStruct((m, XA_HEADS * hd), BF16),
        compiler_params=_cparams(("parallel", "parallel")),
        name="xattn",
    )(qx, kv, kv)


def _router_kernel(x_ref, g_ref, rw_ref, rb_ref, hf_ref, route_ref, gate_ref, cnt_ref, carry_ref, *, tm):
    i = pl.program_id(0)

    @pl.when(i == 0)
    def _():
        carry_ref[...] = jnp.zeros_like(carry_ref)

    hn = _rms(x_ref[...]) * g_ref[...]
    hi, mid = _split2(hn)
    hf_ref[...] = hi.reshape(tm, ROW_SUB, LANES)

    logits = _dot(hi, rw_ref[0]) + (_dot(hi, rw_ref[1]) + _dot(mid, rw_ref[0])) + rb_ref[...]
    lane = lax.broadcasted_iota(I32, (tm, LANES), 1)
    lane_f = lane.astype(F32)
    idxs, vals = [], []
    work = logits
    for _ in range(TOP_K):
        m = jnp.max(work, axis=-1, keepdims=True)
        ik = jnp.min(jnp.where(work == m, lane_f, float(LANES)), axis=-1, keepdims=True)
        idxs.append(ik)
        vals.append(m)
        work = jnp.where(lane_f == ik, -jnp.inf, work)
    es = [jnp.exp(v - vals[0]) for v in vals]
    den = es[0] + es[1] + es[2] + es[3]

    onehot = jnp.zeros((tm, LANES), F32)
    for ik in idxs:
        onehot = onehot + jnp.where(lane_f == ik, 1.0, 0.0)
    r = lax.broadcasted_iota(I32, (tm, tm), 0)
    c = lax.broadcasted_iota(I32, (tm, tm), 1)
    strict = jnp.where(c < r, 1.0, 0.0).astype(BF16)
    before = carry_ref[0:1, :] + _dot(strict, onehot.astype(BF16))
    carry_ref[0:1, :] = carry_ref[0:1, :] + jnp.sum(onehot, axis=0, keepdims=True)
    cnt_ref[...] = jnp.broadcast_to(carry_ref[0:1, :], cnt_ref.shape)

    route = jnp.zeros((tm, LANES), F32)
    gates = jnp.zeros((tm, LANES), F32)
    for kk in range(TOP_K):
        rank = jnp.sum(jnp.where(lane_f == idxs[kk], before, 0.0), axis=-1, keepdims=True)
        route = jnp.where(lane == kk, idxs[kk], route)
        route = jnp.where(lane == TOP_K + kk, rank, route)
        gates = jnp.where(lane == kk, es[kk] / den, gates)
    route_ref[...] = route.astype(I32)
    gate_ref[...] = gates


def _router(h2, g, rw3, rb, tm=512):
    t, d = h2.shape
    return pl.pallas_call(
        functools.partial(_router_kernel, tm=tm),
        grid=(t // tm,),
        in_specs=[
            pl.BlockSpec((tm, d), lambda i: (i, 0)),
            pl.BlockSpec((1, d), lambda i: (0, 0)),
            pl.BlockSpec((3, d, LANES), lambda i: (0, 0, 0)),
            pl.BlockSpec((1, LANES), lambda i: (0, 0)),
        ],
        out_specs=[
            pl.BlockSpec((tm, ROW_SUB, LANES), lambda i: (i, 0, 0)),
            pl.BlockSpec((tm, LANES), lambda i: (i, 0)),
            pl.BlockSpec((tm, LANES), lambda i: (i, 0)),
            pl.BlockSpec((8, LANES), lambda i: (0, 0)),
        ],
        out_shape=[
            jax.ShapeDtypeStruct((t, ROW_SUB, LANES), BF16),
            jax.ShapeDtypeStruct((t, LANES), I32),
            jax.ShapeDtypeStruct((t, LANES), F32),
            jax.ShapeDtypeStruct((8, LANES), F32),
        ],
        scratch_shapes=[pltpu.VMEM((8, LANES), F32)],
        compiler_params=_cparams(("arbitrary",)),
        name="router",
    )(h2, g, rw3, rb)


def _dispatch_kernel(dest_ref, nr_ref, hf_ref, xs_ref, zero_ref, sem, zsem, *, tt, n_blocks):
    @pl.when(pl.program_id(0) == 0)
    def _():
        zero_ref[...] = jnp.zeros_like(zero_ref)

        def fill(start):
            def body(b, carry):
                pad = MOE_TM - nr_ref[b]
                off = b * MOE_TM + nr_ref[b]
                size = MOE_TM
                while size >= 1:
                    hit = (pad & size) != 0
                    cp = pltpu.make_async_copy(zero_ref.at[pl.ds(0, size)], xs_ref.at[pl.ds(off, size)], zsem)

                    @pl.when(hit)
                    def _(cp=cp):
                        cp.start() if start else cp.wait()

                    off = off + jnp.where(hit, size, 0)
                    size //= 2
                return carry
            lax.fori_loop(0, n_blocks, body, 0)

        fill(True)
        fill(False)

    def copy(t, kk):
        return pltpu.make_async_copy(hf_ref.at[t], xs_ref.at[dest_ref[0, 0, t * TOP_K + kk]], sem)

    def issue(t, carry):
        for kk in range(TOP_K):
            copy(t, kk).start()
        return carry

    def drain(t, carry):
        for kk in range(TOP_K):
            copy(t, kk).wait()
        return carry

    lax.fori_loop(0, tt, issue, 0)
    lax.fori_loop(0, tt, drain, 0)


def _dispatch(dest2, nr, hf3, n_blocks, tt):
    t = hf3.shape[0]
    return pl.pallas_call(
        functools.partial(_dispatch_kernel, tt=tt, n_blocks=n_blocks),
        grid=(t // tt,),
        in_specs=[
            pl.BlockSpec((1, 1, tt * TOP_K), lambda i: (i, 0, 0), memory_space=pltpu.SMEM),
            pl.BlockSpec(memory_space=pltpu.SMEM),
            pl.BlockSpec((tt, ROW_SUB, LANES), lambda i: (i, 0, 0)),
        ],
        out_specs=pl.BlockSpec(memory_space=pl.ANY),
        out_shape=jax.ShapeDtypeStruct((n_blocks * MOE_TM, ROW_SUB, LANES), BF16),
        scratch_shapes=[pltpu.VMEM((MOE_TM, ROW_SUB, LANES), BF16), pltpu.SemaphoreType.DMA(()),
                        pltpu.SemaphoreType.DMA(())],
        compiler_params=_cparams(("arbitrary",)),
        name="dispatch",
    )(dest2, nr, hf3)


def _moe_up_kernel(be_ref, bx_ref, nr_ref, first_ref, x_ref, wg_ref, wu_ref, bg_ref, bu_ref, o_ref,
                   wgb_ref, wub_ref):
    i = pl.program_id(1)
    nrows = nr_ref[i]

    @pl.when(first_ref[i] == 1)
    def _():
        wgb_ref[...] = wg_ref[0].astype(BF16)
        wub_ref[...] = wu_ref[0].astype(BF16)

    for h in range(MOE_TM // MOE_SUB):
        rows = slice(h * MOE_SUB, (h + 1) * MOE_SUB)

        @pl.when(nrows > h * MOE_SUB)
        def _(rows=rows):
            xb = x_ref[rows].reshape(MOE_SUB, D_MODEL)
            g = jnp.minimum(_dot(xb, wgb_ref[...]) + bg_ref[0], SWIGLU_LIMIT)
            u = jnp.clip(_dot(xb, wub_ref[...]) + bu_ref[0], -SWIGLU_LIMIT, SWIGLU_LIMIT)
            act = g * jax.nn.sigmoid(SWIGLU_ALPHA * g) * (u + 1.0)
            o_ref[rows, :] = act.astype(o_ref.dtype)

        @pl.when(nrows <= h * MOE_SUB)
        def _(rows=rows):
            o_ref[rows, :] = jnp.zeros((MOE_SUB, MOE_TF), o_ref.dtype)


def _moe_up(meta, xs3, w1, b1, n_blocks):
    be, bx, nr, first = meta
    n_slots = xs3.shape[0]
    nf = D_FF // MOE_TF
    grid_spec = pltpu.PrefetchScalarGridSpec(
        num_scalar_prefetch=4,
        grid=(nf, n_blocks),
        in_specs=[
            pl.BlockSpec((MOE_TM, ROW_SUB, LANES), lambda f, i, be, bx, nr, fi: (bx[i], 0, 0)),
            pl.BlockSpec((1, D_MODEL, MOE_TF), lambda f, i, be, bx, nr, fi: (be[i], 0, f)),
            pl.BlockSpec((1, D_MODEL, MOE_TF), lambda f, i, be, bx, nr, fi: (be[i], 0, nf + f)),
            pl.BlockSpec((1, 1, MOE_TF), lambda f, i, be, bx, nr, fi: (be[i], 0, f)),
            pl.BlockSpec((1, 1, MOE_TF), lambda f, i, be, bx, nr, fi: (be[i], 0, nf + f)),
        ],
        out_specs=pl.BlockSpec((MOE_TM, MOE_TF), lambda f, i, be, bx, nr, fi: (i, f)),
        scratch_shapes=[pltpu.VMEM((D_MODEL, MOE_TF), BF16), pltpu.VMEM((D_MODEL, MOE_TF), BF16)],
    )
    return pl.pallas_call(
        _moe_up_kernel,
        grid_spec=grid_spec,
        out_shape=jax.ShapeDtypeStruct((n_slots, D_FF), BF16),
        compiler_params=_cparams(("arbitrary", "arbitrary")),
        name="moe_up",
    )(be, bx, nr, first, xs3, w1, w1, b1, b1)


def _moe_down_kernel(be_ref, bx_ref, nr_ref, first_ref, a_ref, w_ref, b_ref, o_ref, wb_ref):
    i = pl.program_id(0)
    nrows = nr_ref[i]

    @pl.when(first_ref[i] == 1)
    def _():
        wb_ref[...] = w_ref[0].astype(BF16)

    for h in range(MOE_TM // MOE_SUB):
        rows = slice(h * MOE_SUB, (h + 1) * MOE_SUB)

        @pl.when(nrows > h * MOE_SUB)
        def _(rows=rows):
            y = _dot(a_ref[rows, :], wb_ref[...]) + b_ref[0]
            o_ref[rows] = y.astype(o_ref.dtype).reshape(MOE_SUB, ROW_SUB, LANES)

        @pl.when(nrows <= h * MOE_SUB)
        def _(rows=rows):
            o_ref[rows] = jnp.zeros((MOE_SUB, ROW_SUB, LANES), o_ref.dtype)


def _moe_down(meta, act, w2, b2, n_blocks):
    be, bx, nr, first = meta
    n_slots = act.shape[0]
    grid_spec = pltpu.PrefetchScalarGridSpec(
        num_scalar_prefetch=4,
        grid=(n_blocks,),
        in_specs=[
            pl.BlockSpec((MOE_TM, D_FF), lambda i, be, bx, nr, fi: (bx[i], 0)),
            pl.BlockSpec((1, D_FF, D_MODEL), lambda i, be, bx, nr, fi: (be[i], 0, 0)),
            pl.BlockSpec((1, 1, D_MODEL), lambda i, be, bx, nr, fi: (be[i], 0, 0)),
        ],
        out_specs=pl.BlockSpec((MOE_TM, ROW_SUB, LANES), lambda i, be, bx, nr, fi: (i, 0, 0)),
        scratch_shapes=[pltpu.VMEM((D_FF, D_MODEL), BF16)],
    )
    return pl.pallas_call(
        _moe_down_kernel,
        grid_spec=grid_spec,
        out_shape=jax.ShapeDtypeStruct((n_slots, ROW_SUB, LANES), BF16),
        compiler_params=_cparams(("arbitrary",)),
        name="moe_down",
    )(be, bx, nr, first, act, w2, b2)


def _combine_kernel(dest_ref, y_ref, gate_ref, h_ref, o_ref, buf_ref, sem, *, tc):
    def copy(t, kk):
        return pltpu.make_async_copy(y_ref.at[dest_ref[0, 0, t * TOP_K + kk]], buf_ref.at[kk * tc + t], sem)

    def issue(t, carry):
        for kk in range(TOP_K):
            copy(t, kk).start()
        return carry

    def drain(t, carry):
        for kk in range(TOP_K):
            copy(t, kk).wait()
        return carry

    lax.fori_loop(0, tc, issue, 0)
    lax.fori_loop(0, tc, drain, 0)

    gates = gate_ref[...]
    acc = h_ref[...]
    for kk in range(TOP_K):
        yk = buf_ref[kk * tc:(kk + 1) * tc].reshape(tc, ROW_SUB * LANES).astype(F32)
        acc = acc + gates[:, kk:kk + 1] * yk
    o_ref[...] = acc


def _combine(dest2, y3, gates, h2, tc):
    t, d = h2.shape
    return pl.pallas_call(
        functools.partial(_combine_kernel, tc=tc),
        grid=(t // tc,),
        in_specs=[
            pl.BlockSpec((1, 1, tc * TOP_K), lambda i: (i, 0, 0), memory_space=pltpu.SMEM),
            pl.BlockSpec(memory_space=pl.ANY),
            pl.BlockSpec((tc, LANES), lambda i: (i, 0)),
            pl.BlockSpec((tc, d), lambda i: (i, 0)),
        ],
        out_specs=pl.BlockSpec((tc, d), lambda i: (i, 0)),
        out_shape=jax.ShapeDtypeStruct((t, d), F32),
        scratch_shapes=[pltpu.VMEM((TOP_K * tc, ROW_SUB, LANES), BF16), pltpu.SemaphoreType.DMA(())],
        compiler_params=_cparams(("arbitrary",)),
        name="combine",
    )(dest2, y3, gates, h2)


def _moe_plan(counts, route, n_blocks):
    idx = route[:, :TOP_K]
    rank = route[:, TOP_K:2 * TOP_K]
    nblk = (counts + MOE_TM - 1) // MOE_TM
    blk_end = jnp.cumsum(nblk)
    blk_start = blk_end - nblk
    dest = (blk_start * MOE_TM)[idx] + rank
    total = blk_end[-1]
    ids = jnp.arange(n_blocks, dtype=I32)
    live = ids < total
    src = jnp.minimum(ids, total - 1)
    be = jnp.minimum(jnp.sum(blk_end[None, :] <= src[:, None], axis=1), N_EXPERTS - 1).astype(I32)
    nr = jnp.where(live, jnp.clip(counts[be] - (ids - blk_start[be]) * MOE_TM, 0, MOE_TM), 0).astype(I32)
    first = jnp.logical_and(live, ids == blk_start[be]).astype(I32)
    return dest.astype(I32), (be, src.astype(I32), nr, first)


def _layer(h, mem, lam_init, norm_mix_g, w_in, conv_w, conv_b, dt_bias, a_log, d_skip, ssd_norm_g,
           da_q_norm_g, da_k_norm_g, lq1, lk1, lq2, lk2, da_subln_g, w_out, norm_xa_g, norm_mem_g,
           xa_wq, xa_wkv, xa_q_norm_g, xa_k_norm_g, xa_wo, norm_ffn_g, router_w, router_b,
           moe_w1, moe_b1, moe_w2, moe_b2):
    bsz, seq, d = h.shape
    t = bsz * seq
    mem_len = mem.shape[1]
    row = lambda v: v.astype(F32).reshape(1, -1)
    x2 = h.reshape(t, d)

    o1 = SSD_WIDTH + XBC_WIDTH
    o2 = o1 + SSD_HEADS
    w_main = jnp.concatenate([w_in[:, :o1], w_in[:, o2:]], axis=1).astype(BF16)
    w_dt = _split3_host(jnp.pad(w_in[:, o1:o2], ((0, 0), (0, LANES - SSD_HEADS))))
    proj, dt_raw = _in_proj(x2, row(norm_mix_g), w_main, w_dt)
    proj3 = proj.reshape(bsz, seq, PROJ_COLS)
    y_ssd = _ssd(proj3, dt_raw.reshape(bsz, seq, LANES), conv_w, conv_b, dt_bias, a_log, d_skip, ssd_norm_g)

    gq = jnp.tile(da_q_norm_g.astype(F32), 2) * (DA_HALF ** -0.5)
    gk = jnp.tile(da_k_norm_g.astype(F32), 2)
    gains = jnp.concatenate([jnp.tile(gq[None], (DA_HEADS, 1)), jnp.tile(gk[None], (DA_HEADS, 1))])
    qk = _qk_norm(proj, gains.reshape(2 * DA_HEADS, 1, LANES))
    lvec = jnp.stack([lq1, lk1, lq2, lk2]).astype(F32)
    y_da = _diff_attn(qk.reshape(bsz, seq, 2 * DA_WIDTH), proj3, lvec, row(da_subln_g), lam_init)

    h1 = _mm_res([y_ssd.reshape(t, SSD_WIDTH), y_da.reshape(t, DA_WIDTH)], w_out.astype(BF16), x2)

    eq = jnp.tile(xa_q_norm_g.astype(F32) * (XA_HEAD_DIM ** -0.5), XA_HEADS).reshape(1, -1)
    qx = _norm_proj(h1, row(norm_xa_g), xa_wq.astype(BF16), eq[:, :XA_HEAD_DIM], XA_HEADS, 1024, XA_HEAD_DIM)
    kv = _norm_proj(mem.reshape(bsz * mem_len, d), row(norm_mem_g), xa_wkv.astype(BF16),
                    row(xa_k_norm_g), XA_HEADS, bsz * mem_len, XA_HEAD_DIM)
    ox = _xattn(qx, kv, seq, mem_len)
    h2 = _mm_res([ox], xa_wo.astype(BF16), h1)

    rw3 = _split3_host(jnp.pad(router_w, ((0, 0), (0, LANES - N_EXPERTS))))
    rb = jnp.pad(router_b.astype(F32), (0, LANES - N_EXPERTS), constant_values=-jnp.inf).reshape(1, LANES)
    hf, route, gates, cnt = _router(h2, row(norm_ffn_g), rw3, rb)
    n_blocks = (t * TOP_K) // MOE_TM + N_EXPERTS
    n_slots = n_blocks * MOE_TM
    dest, meta = _moe_plan(cnt[0, :N_EXPERTS].astype(I32), route, n_blocks)
    tt = 512
    xs = _dispatch(dest.reshape(t // tt, 1, tt * TOP_K), meta[2], hf, n_blocks, tt)
    act = _moe_up(meta, xs, moe_w1,
                  moe_b1.reshape(N_EXPERTS, 1, 2 * D_FF), n_blocks)
    y = _moe_down(meta, act, moe_w2, moe_b2.reshape(N_EXPERTS, 1, D_MODEL), n_blocks)
    tc = 256
    out = _combine(dest.reshape(t // tc, 1, tc * TOP_K), y, gates, h2, tc)
    return out.reshape(bsz, seq, d)


def kernel(x, mem, norm_mix_g, w_in, conv_w, conv_b, dt_bias, a_log, d_skip, ssd_norm_g, da_q_norm_g,
           da_k_norm_g, lambda_q1, lambda_k1, lambda_q2, lambda_k2, da_subln_g, w_out, norm_xa_g,
           norm_mem_g, xa_wq, xa_wkv, xa_q_norm_g, xa_k_norm_g, xa_wo, norm_ffn_g, router_w, router_b,
           moe_w1, moe_b1, moe_w2, moe_b2):
    h = x
    for layer in range(norm_mix_g.shape[0]):
        lam_init = 0.8 - 0.6 * math.exp(-0.3 * layer)
        h = _layer(h, mem, lam_init, norm_mix_g[layer], w_in[layer], conv_w[layer], conv_b[layer],
                   dt_bias[layer], a_log[layer], d_skip[layer], ssd_norm_g[layer], da_q_norm_g[layer],
                   da_k_norm_g[layer], lambda_q1[layer], lambda_k1[layer], lambda_q2[layer],
                   lambda_k2[layer], da_subln_g[layer], w_out[layer], norm_xa_g[layer], norm_mem_g[layer],
                   xa_wq[layer], xa_wkv[layer], xa_q_norm_g[layer], xa_k_norm_g[layer], xa_wo[layer],
                   norm_ffn_g[layer], router_w[layer], router_b[layer], moe_w1[layer], moe_b1[layer],
                   moe_w2[layer], moe_b2[layer])
    return h
```

```python
import functools
import math

import jax
import jax.numpy as jnp
from jax import lax
from jax.experimental import pallas as pl
from jax.experimental.pallas import tpu as pltpu

F32 = jnp.float32
BF16 = jnp.bfloat16
I32 = jnp.int32

D_MODEL = 2048
SSD_WIDTH = 1024
SSD_HEAD_DIM = 64
SSD_HEADS = 16
SSD_GROUPS = 2
SSD_STATE = 128
SSD_CONV = 4
SSD_CHUNK = 128
XBC_WIDTH = SSD_WIDTH + 2 * SSD_GROUPS * SSD_STATE
DA_WIDTH = 1024
DA_HEAD_DIM = 128
DA_HALF = 64
DA_HEADS = 8
XA_HEADS = 4
XA_HEAD_DIM = D_MODEL // XA_HEADS
N_EXPERTS = 32
TOP_K = 4
D_FF = 2048
SWIGLU_LIMIT = 7.0
SWIGLU_ALPHA = 1.702
EPS = 1e-6

LANES = 128
ROW_SUB = D_MODEL // LANES
VMEM_LIMIT = 56 * 1024 * 1024

PROJ_COLS = SSD_WIDTH + XBC_WIDTH + 3 * DA_WIDTH
COL_Q = (SSD_WIDTH + XBC_WIDTH) // LANES
COL_V = COL_Q + 2 * DA_HEADS

MOE_TM = 512
MOE_SUB = 256
MOE_TF = 1024


def _dot(a, b):
    return jnp.dot(a, b, preferred_element_type=F32)


def _dot_nt(a, b):
    return lax.dot_general(a, b, (((1,), (1,)), ((), ())), preferred_element_type=F32)


def _dot_tn(a, b):
    return lax.dot_general(a, b, (((0,), (0,)), ((), ())), preferred_element_type=F32)


def _rms(x):
    return x * lax.rsqrt(jnp.mean(x * x, axis=-1, keepdims=True) + EPS)


def _split2(x):
    hi = x.astype(BF16)
    mid = (x - hi.astype(F32)).astype(BF16)
    return hi, mid


def _split3(x):
    hi = x.astype(BF16)
    r = x - hi.astype(F32)
    mid = r.astype(BF16)
    lo = (r - mid.astype(F32)).astype(BF16)
    return hi, mid, lo


def _split3_host(w):
    hi, mid, lo = _split3(w.astype(F32))
    return jnp.stack([hi, mid, lo])


def _silu(x):
    return x * jax.nn.sigmoid(x)


def _cparams(sem):
    return pltpu.CompilerParams(dimension_semantics=sem, vmem_limit_bytes=VMEM_LIMIT)


def _w_prep_kernel(w_ref, o_ref):
    x = w_ref[...]
    o1 = SSD_WIDTH + XBC_WIDTH
    o_ref[...] = jnp.concatenate([x[:, :o1], x[:, o1 + SSD_HEADS:]], axis=1).astype(o_ref.dtype)


def _w_prep(w_in, tr=256):
    k, n = w_in.shape
    return pl.pallas_call(
        _w_prep_kernel,
        grid=(k // tr,),
        in_specs=[pl.BlockSpec((tr, n), lambda i: (i, 0))],
        out_specs=pl.BlockSpec((tr, PROJ_COLS), lambda i: (i, 0)),
        out_shape=jax.ShapeDtypeStruct((k, PROJ_COLS), BF16),
        compiler_params=_cparams(("parallel",)),
        name="w_prep",
    )(w_in)


def _in_proj_kernel(x_ref, g_ref, w_ref, wdt_ref, o_ref, dt_ref, xn_ref):
    @pl.when(pl.program_id(1) == 0)
    def _():
        hn = _rms(x_ref[...]) * g_ref[...]
        hi, mid = _split2(hn)
        xn_ref[...] = hi
        dt_ref[...] = _dot(hi, wdt_ref[0]) + (_dot(hi, wdt_ref[1]) + _dot(mid, wdt_ref[0]))

    o_ref[...] = _dot(xn_ref[...], w_ref[...]).astype(o_ref.dtype)


def _in_proj(x2, g, w, wdt3, tm=1024, tn=512):
    m, k = x2.shape
    n = w.shape[1]
    return pl.pallas_call(
        _in_proj_kernel,
        grid=(m // tm, n // tn),
        in_specs=[
            pl.BlockSpec((tm, k), lambda i, j: (i, 0)),
            pl.BlockSpec((1, k), lambda i, j: (0, 0)),
            pl.BlockSpec((k, tn), lambda i, j: (0, j)),
            pl.BlockSpec((3, k, LANES), lambda i, j: (0, 0, 0)),
        ],
        out_specs=[
            pl.BlockSpec((tm, tn), lambda i, j: (i, j)),
            pl.BlockSpec((tm, LANES), lambda i, j: (i, 0)),
        ],
        out_shape=[jax.ShapeDtypeStruct((m, n), BF16), jax.ShapeDtypeStruct((m, LANES), F32)],
        scratch_shapes=[pltpu.VMEM((tm, k), BF16)],
        compiler_params=_cparams(("parallel", "arbitrary")),
        name="in_proj",
    )(x2, g, w, wdt3)


def _norm_proj_kernel(x_ref, g_ref, w_ref, eg_ref, o_ref, xn_ref, *, epi_tiles):
    j = pl.program_id(1)

    @pl.when(j == 0)
    def _():
        xn_ref[...] = (_rms(x_ref[...]) * g_ref[...]).astype(BF16)

    acc = _dot(xn_ref[...], w_ref[...])

    @pl.when(j < epi_tiles)
    def _():
        o_ref[...] = (_rms(acc) * eg_ref[...]).astype(o_ref.dtype)

    @pl.when(j >= epi_tiles)
    def _():
        o_ref[...] = acc.astype(o_ref.dtype)


def _norm_proj(x2, g, w, eg, epi_tiles, tm, tn):
    m, k = x2.shape
    n = w.shape[1]
    return pl.pallas_call(
        functools.partial(_norm_proj_kernel, epi_tiles=epi_tiles),
        grid=(m // tm, n // tn),
        in_specs=[
            pl.BlockSpec((tm, k), lambda i, j: (i, 0)),
            pl.BlockSpec((1, k), lambda i, j: (0, 0)),
            pl.BlockSpec((k, tn), lambda i, j: (0, j)),
            pl.BlockSpec((1, tn), lambda i, j: (0, 0)),
        ],
        out_specs=pl.BlockSpec((tm, tn), lambda i, j: (i, j)),
        out_shape=jax.ShapeDtypeStruct((m, n), BF16),
        scratch_shapes=[pltpu.VMEM((tm, k), BF16)],
        compiler_params=_cparams(("parallel", "arbitrary")),
        name="norm_proj",
    )(x2, g, w, eg)


def _mm_res_kernel(*refs, n_a):
    a_refs, w_refs = refs[:n_a], refs[n_a:2 * n_a]
    res_ref, o_ref = refs[2 * n_a], refs[2 * n_a + 1]
    acc = res_ref[...]
    for a, w in zip(a_refs, w_refs):
        acc = acc + _dot(a[...], w[...])
    o_ref[...] = acc


def _mm_res(a_list, w, res, tm=1024, tn=1024):
    m, n = res.shape
    n_a = len(a_list)
    in_specs = []
    for a in a_list:
        in_specs.append(pl.BlockSpec((tm, a.shape[1]), lambda i, j: (i, 0)))
    for idx, a in enumerate(a_list):
        in_specs.append(pl.BlockSpec((a.shape[1], tn), lambda i, j, idx=idx: (idx, j)))
    in_specs.append(pl.BlockSpec((tm, tn), lambda i, j: (i, j)))
    return pl.pallas_call(
        functools.partial(_mm_res_kernel, n_a=n_a),
        grid=(m // tm, n // tn),
        in_specs=in_specs,
        out_specs=pl.BlockSpec((tm, tn), lambda i, j: (i, j)),
        out_shape=jax.ShapeDtypeStruct((m, n), F32),
        compiler_params=_cparams(("parallel", "arbitrary")),
        name="mm_res",
    )(*a_list, *([w] * n_a), res)


def _ssd_kernel(z_ref, xs_ref, b_ref, c_ref, dtr_ref, cw_ref, cb_ref, dtb_ref, alog_ref,
                dsk_ref, ng_ref, sel_ref, o_ref, ext_ref, ht_ref, y_ref, *, nb):
    L = SSD_CHUNK

    @pl.when(pl.program_id(1) == 0)
    def _():
        ht_ref[...] = jnp.zeros_like(ht_ref)
        ext_ref[:, 0:8, :] = jnp.zeros((nb, 8, XBC_WIDTH), F32)

    row = lax.broadcasted_iota(I32, (L, L), 0)
    col = lax.broadcasted_iota(I32, (L, L), 1)
    causal = col <= row
    tril = jnp.where(causal, 1.0, 0.0).astype(BF16)
    lane = lax.broadcasted_iota(I32, (L, LANES), 1)
    first = lane < SSD_HEAD_DIM
    a = -jnp.exp(alog_ref[...])

    for bb in range(nb):
        _ssd_chunk(bb, z_ref, xs_ref, b_ref, c_ref, dtr_ref, cw_ref, cb_ref, dtb_ref, a, dsk_ref, ng_ref,
                   sel_ref, o_ref, ext_ref, ht_ref, y_ref, causal, tril, first)


def _ssd_chunk(bb, z_ref, xs_ref, b_ref, c_ref, dtr_ref, cw_ref, cb_ref, dtb_ref, a, dsk_ref, ng_ref,
               sel_ref, o_ref, ext_ref, ht_ref, y_ref, causal, tril, first):
    L = SSD_CHUNK
    n_pairs = SSD_HEADS // 2
    pairs_per_group = n_pairs // SSD_GROUPS
    c_off = SSD_WIDTH + SSD_GROUPS * SSD_STATE
    ext = ext_ref.at[bb]
    ext[8:8 + L, 0:SSD_WIDTH] = xs_ref[bb].astype(F32)
    ext[8:8 + L, SSD_WIDTH:c_off] = b_ref[bb].astype(F32)
    ext[8:8 + L, c_off:XBC_WIDTH] = c_ref[bb].astype(F32)
    acc = jnp.broadcast_to(cb_ref[...], (L, XBC_WIDTH))
    for k in range(SSD_CONV):
        acc = acc + cw_ref[k:k + 1, :] * ext[pl.ds(8 - (SSD_CONV - 1) + k, L), :]
    ext[0:8, :] = ext[L:L + 8, :]
    xbc = _silu(acc)

    dt = jax.nn.softplus(dtr_ref[bb] + dtb_ref[...])
    adt = dt * a
    a_hi, a_mid, a_lo = _split3(adt)
    acum = _dot(tril, a_hi) + (_dot(tril, a_mid) + _dot(tril, a_lo))
    acum_t = acum.T
    pieces = [jnp.concatenate([p, q], axis=0) for p, q in zip(_split3(acum), _split3(dt))]

    cbs = []
    for g in range(SSD_GROUPS):
        bg = xbc[:, SSD_WIDTH + g * SSD_STATE:SSD_WIDTH + (g + 1) * SSD_STATE].astype(BF16)
        cg = xbc[:, c_off + g * SSD_STATE:c_off + (g + 1) * SSD_STATE].astype(BF16)
        cbs.append((bg, cg, _dot_nt(cg, bg)))

    for p in range(n_pairs):
        bg, cg, cb = cbs[p // pairs_per_group]
        sel = sel_ref[:, 2 * LANES * p:2 * LANES * (p + 1)]
        fb = _dot(pieces[0], sel) + (_dot(pieces[1], sel) + _dot(pieces[2], sel))
        ms = []
        for i in range(2):
            h = 2 * p + i
            seg = fb[:L, i * LANES:(i + 1) * LANES] - acum_t[h:h + 1, :]
            dec = jnp.exp(jnp.where(causal, seg, -jnp.inf))
            ms.append((cb * dec).astype(BF16))
        ac = jnp.where(first, fb[:L, :LANES], fb[:L, LANES:])
        dtp = jnp.where(first, fb[L:, :LANES], fb[L:, LANES:])
        xp = xbc[:, p * LANES:(p + 1) * LANES]
        xdt = xp * dtp
        y = _dot(ms[0], jnp.where(first, xdt, 0.0).astype(BF16))
        y = y + _dot(ms[1], jnp.where(first, 0.0, xdt).astype(BF16))
        ht = ht_ref[bb * n_pairs + p]
        y = y + _dot(cg, ht.astype(BF16)) * jnp.exp(ac)
        a_last = ac[L - 1:L, :]
        ht_ref[bb * n_pairs + p] = (ht * jnp.exp(a_last)
                                    + _dot_tn(bg, (xdt * jnp.exp(a_last - ac)).astype(BF16)))
        y_ref[bb, :, p * LANES:(p + 1) * LANES] = y + xp * dsk_ref[:, p * LANES:(p + 1) * LANES]

    yg = y_ref[bb] * _silu(z_ref[bb].astype(F32))
    gw = SSD_WIDTH // SSD_GROUPS
    for g in range(SSD_GROUPS):
        blk = _rms(yg[:, g * gw:(g + 1) * gw]) * ng_ref[:, g * gw:(g + 1) * gw]
        o_ref[bb, :, g * gw:(g + 1) * gw] = blk.astype(o_ref.dtype)


def _ssd(proj3, dt3, conv_w, conv_b, dt_bias, a_log, d_skip, norm_g, nb=2):
    bsz, s, _ = proj3.shape
    L = SSD_CHUNK
    pad = LANES - SSD_HEADS
    dtb = jnp.pad(dt_bias.astype(F32), (0, pad)).reshape(1, LANES)
    alog = jnp.pad(a_log.astype(F32), (0, pad)).reshape(1, LANES)
    dsk = jnp.repeat(d_skip.astype(F32), SSD_HEAD_DIM).reshape(1, SSD_WIDTH)
    sel = (jnp.arange(LANES)[:, None] == (jnp.arange(SSD_HEADS * LANES)[None, :] // LANES)).astype(BF16)
    const = lambda shape: pl.BlockSpec(shape, lambda b, c: (0,) * len(shape))
    return pl.pallas_call(
        functools.partial(_ssd_kernel, nb=nb),
        grid=(bsz // nb, s // L),
        in_specs=[
            pl.BlockSpec((nb, L, SSD_WIDTH), lambda b, c: (b, c, 0)),
            pl.BlockSpec((nb, L, SSD_WIDTH), lambda b, c: (b, c, 1)),
            pl.BlockSpec((nb, L, 2 * SSD_STATE), lambda b, c: (b, c, 2 * SSD_WIDTH // (2 * SSD_STATE))),
            pl.BlockSpec((nb, L, 2 * SSD_STATE), lambda b, c: (b, c, 2 * SSD_WIDTH // (2 * SSD_STATE) + 1)),
            pl.BlockSpec((nb, L, LANES), lambda b, c: (b, c, 0)),
            const((SSD_CONV, XBC_WIDTH)), const((1, XBC_WIDTH)), const((1, LANES)), const((1, LANES)),
            const((1, SSD_WIDTH)), const((1, SSD_WIDTH)), const((LANES, SSD_HEADS * LANES)),
        ],
        out_specs=pl.BlockSpec((nb, L, SSD_WIDTH), lambda b, c: (b, c, 0)),
        out_shape=jax.ShapeDtypeStruct((bsz, s, SSD_WIDTH), BF16),
        scratch_shapes=[
            pltpu.VMEM((nb, L + 8, XBC_WIDTH), F32),
            pltpu.VMEM((nb * SSD_HEADS // 2, SSD_STATE, LANES), F32),
            pltpu.VMEM((nb, L, SSD_WIDTH), F32),
        ],
        compiler_params=_cparams(("parallel", "arbitrary")),
        name="ssd",
    )(proj3, proj3, proj3, proj3, dt3, conv_w.astype(F32), conv_b.astype(F32).reshape(1, XBC_WIDTH),
      dtb, alog, dsk, norm_g.astype(F32).reshape(1, SSD_WIDTH), sel)


def _half_norm(x, g):
    ra = lax.broadcasted_iota(I32, (LANES, LANES), 0) // DA_HALF
    rb = lax.broadcasted_iota(I32, (LANES, LANES), 1) // DA_HALF
    grp = jnp.where(ra == rb, 1.0, 0.0).astype(BF16)
    hi, mid = _split2(x * x)
    ss = _dot(hi, grp) + _dot(mid, grp)
    return x * lax.rsqrt(ss * (1.0 / DA_HALF) + EPS) * g


def _da_kernel(q_ref, k_ref, v_ref, gq_ref, gk_ref, lv_ref, sg_ref, o_ref,
               kn_ref, va_ref, q2_ref, m_ref, l_ref, acc_ref, *, lam_init, tq, seq):
    qi = pl.program_id(2)
    tk = tq

    @pl.when(qi == 0)
    def _():
        for c in range(seq // tk):
            rows = slice(c * tk, (c + 1) * tk)
            kn_ref[rows, :] = _half_norm(k_ref[0, rows, :].astype(F32), gk_ref[...]).astype(BF16)
            va_ref[rows, :LANES] = v_ref[0, rows, :]
            va_ref[rows, LANES:] = jnp.ones((tk, LANES), BF16)

    q = _half_norm(q_ref[0].astype(F32), gq_ref[...]).astype(BF16)
    first = lax.broadcasted_iota(I32, q.shape, 1) < DA_HALF
    zero = jnp.zeros_like(q)
    q2_ref[0] = jnp.where(first, q, zero)
    q2_ref[1] = jnp.where(first, zero, q)
    m_ref[...] = jnp.full(m_ref.shape, -jnp.inf, F32)
    l_ref[...] = jnp.zeros_like(l_ref)
    acc_ref[...] = jnp.zeros_like(acc_ref)

    def step(c, masked):
        k = kn_ref[c * tk:(c + 1) * tk, :]
        va = va_ref[c * tk:(c + 1) * tk, :]
        if masked:
            keep = (lax.broadcasted_iota(I32, (tq, tk), 1) <= lax.broadcasted_iota(I32, (tq, tk), 0))
        for j in range(2):
            s = _dot_nt(q2_ref[j], k)
            if masked:
                s = jnp.where(keep, s, -jnp.inf)
            m_prev = m_ref[j]
            m_new = jnp.maximum(m_prev, jnp.max(s, axis=-1, keepdims=True))
            alpha = jnp.exp(m_prev - m_new)
            p = jnp.exp(s - jnp.concatenate([m_new] * (tk // LANES), axis=1))
            pv = _dot(p.astype(BF16), va)
            l_ref[j] = alpha * l_ref[j] + pv[:, LANES:]
            acc_ref[j] = alpha * acc_ref[j] + pv[:, :LANES]
            m_ref[j] = m_new

    for qv in range(seq // tq):
        @pl.when(qi == qv)
        def _(qv=qv):
            for c in range(qv):
                step(c, False)
            step(qv, True)

    lv = lv_ref[...]
    lam = (jnp.exp(jnp.sum(lv[0:1] * lv[1:2], axis=-1, keepdims=True))
           - jnp.exp(jnp.sum(lv[2:3] * lv[3:4], axis=-1, keepdims=True)) + lam_init)
    o = acc_ref[0] / l_ref[0] - lam * (acc_ref[1] / l_ref[1])
    o_ref[0] = (_rms(o) * (sg_ref[...] * (1.0 - lam_init))).astype(o_ref.dtype)


def _diff_attn(proj3, gq, gk, lvec, subln_g, lam_init, tq=512):
    bsz, s, _ = proj3.shape
    const = lambda shape: pl.BlockSpec(shape, lambda b, h, qi: (0,) * len(shape))
    return pl.pallas_call(
        functools.partial(_da_kernel, lam_init=lam_init, tq=tq, seq=s),
        grid=(bsz, DA_HEADS, s // tq),
        in_specs=[
            pl.BlockSpec((1, tq, LANES), lambda b, h, qi: (b, qi, COL_Q + h)),
            pl.BlockSpec((1, s, LANES), lambda b, h, qi: (b, 0, COL_Q + DA_HEADS + h)),
            pl.BlockSpec((1, s, LANES), lambda b, h, qi: (b, 0, COL_V + h)),
            const((1, LANES)), const((1, LANES)), const((4, DA_HALF)), const((1, LANES)),
        ],
        out_specs=pl.BlockSpec((1, tq, LANES), lambda b, h, qi: (b, qi, h)),
        out_shape=jax.ShapeDtypeStruct((bsz, s, DA_WIDTH), BF16),
        scratch_shapes=[
            pltpu.VMEM((s, LANES), BF16),
            pltpu.VMEM((s, 2 * LANES), BF16),
            pltpu.VMEM((2, tq, LANES), BF16),
            pltpu.VMEM((2, tq, LANES), F32),
            pltpu.VMEM((2, tq, LANES), F32),
            pltpu.VMEM((2, tq, LANES), F32),
        ],
        compiler_params=_cparams(("parallel", "parallel", "arbitrary")),
        name="diff_attn",
    )(proj3, proj3, proj3, gq, gk, lvec, subln_g)


def _xattn_kernel(q_ref, k_ref, v_ref, o_ref):
    s = _dot_nt(q_ref[...], k_ref[...])
    p = jnp.exp(s - jnp.max(s, axis=-1, keepdims=True))
    o = _dot(p.astype(BF16), v_ref[...]) / jnp.sum(p, axis=-1, keepdims=True)
    o_ref[...] = o.astype(o_ref.dtype)


def _xattn(qx, kv, seq, mem_len, tm=512):
    m = qx.shape[0]
    hd = XA_HEAD_DIM
    return pl.pallas_call(
        _xattn_kernel,
        grid=(m // tm, XA_HEADS),
        in_specs=[
            pl.BlockSpec((tm, hd), lambda i, j: (i, j)),
            pl.BlockSpec((mem_len, hd), lambda i, j: ((i * tm) // seq, j)),
            pl.BlockSpec((mem_len, hd), lambda i, j: ((i * tm) // seq, XA_HEADS + j)),
        ],
        out_specs=pl.BlockSpec((tm, hd), lambda i, j: (i, j)),
        out_shape=jax.ShapeDtypeStruct((m, XA_HEADS * hd), BF16),
        compiler_params=_cparams(("parallel", "parallel")),
        name="xattn",
    )(qx, kv, kv)


def _router_kernel(x_ref, g_ref, rw_ref, rb_ref, hf_ref, route_ref, gate_ref, cnt_ref, carry_ref, *, tm):
    i = pl.program_id(0)

    @pl.when(i == 0)
    def _():
        carry_ref[...] = jnp.zeros_like(carry_ref)

    hn = _rms(x_ref[...]) * g_ref[...]
    hi, mid = _split2(hn)
    hf_ref[...] = hi.reshape(tm, ROW_SUB, LANES)

    logits = _dot(hi, rw_ref[0]) + (_dot(hi, rw_ref[1]) + _dot(mid, rw_ref[0])) + rb_ref[...]
    lane = lax.broadcasted_iota(I32, (tm, LANES), 1)
    lane_f = lane.astype(F32)
    idxs, vals = [], []
    work = logits
    for _ in range(TOP_K):
        m = jnp.max(work, axis=-1, keepdims=True)
        ik = jnp.min(jnp.where(work == m, lane_f, float(LANES)), axis=-1, keepdims=True)
        idxs.append(ik)
        vals.append(m)
        work = jnp.where(lane_f == ik, -jnp.inf, work)
    es = [jnp.exp(v - vals[0]) for v in vals]
    den = es[0] + es[1] + es[2] + es[3]

    onehot = jnp.zeros((tm, LANES), F32)
    for ik in idxs:
        onehot = onehot + jnp.where(lane_f == ik, 1.0, 0.0)
    r = lax.broadcasted_iota(I32, (tm, tm), 0)
    c = lax.broadcasted_iota(I32, (tm, tm), 1)
    strict = jnp.where(c < r, 1.0, 0.0).astype(BF16)
    before = carry_ref[0:1, :] + _dot(strict, onehot.astype(BF16))
    carry_ref[0:1, :] = carry_ref[0:1, :] + jnp.sum(onehot, axis=0, keepdims=True)
    cnt_ref[...] = jnp.broadcast_to(carry_ref[0:1, :], cnt_ref.shape)

    route = jnp.zeros((tm, LANES), F32)
    gates = jnp.zeros((tm, LANES), F32)
    for kk in range(TOP_K):
        rank = jnp.sum(jnp.where(lane_f == idxs[kk], before, 0.0), axis=-1, keepdims=True)
        route = jnp.where(lane == kk, idxs[kk], route)
        route = jnp.where(lane == TOP_K + kk, rank, route)
        gates = jnp.where(lane == kk, es[kk] / den, gates)
    route_ref[...] = route.astype(I32)
    gate_ref[...] = gates


def _router(h2, g, rw3, rb, tm=512):
    t, d = h2.shape
    return pl.pallas_call(
        functools.partial(_router_kernel, tm=tm),
        grid=(t // tm,),
        in_specs=[
            pl.BlockSpec((tm, d), lambda i: (i, 0)),
            pl.BlockSpec((1, d), lambda i: (0, 0)),
            pl.BlockSpec((3, d, LANES), lambda i: (0, 0, 0)),
            pl.BlockSpec((1, LANES), lambda i: (0, 0)),
        ],
        out_specs=[
            pl.BlockSpec((tm, ROW_SUB, LANES), lambda i: (i, 0, 0)),
            pl.BlockSpec((tm, LANES), lambda i: (i, 0)),
            pl.BlockSpec((tm, LANES), lambda i: (i, 0)),
            pl.BlockSpec((8, LANES), lambda i: (0, 0)),
        ],
        out_shape=[
            jax.ShapeDtypeStruct((t, ROW_SUB, LANES), BF16),
            jax.ShapeDtypeStruct((t, LANES), I32),
            jax.ShapeDtypeStruct((t, LANES), F32),
            jax.ShapeDtypeStruct((8, LANES), F32),
        ],
        scratch_shapes=[pltpu.VMEM((8, LANES), F32)],
        compiler_params=_cparams(("arbitrary",)),
        name="router",
    )(h2, g, rw3, rb)


def _dispatch_kernel(dest_ref, nr_ref, hf_ref, xs_ref, zero_ref, sem, zsem, *, tt, n_blocks):
    @pl.when(pl.program_id(0) == 0)
    def _():
        zero_ref[...] = jnp.zeros_like(zero_ref)

        def fill(start):
            def body(b, carry):
                pad = MOE_TM - nr_ref[b]
                off = b * MOE_TM + nr_ref[b]
                size = MOE_TM
                while size >= 1:
                    hit = (pad & size) != 0
                    cp = pltpu.make_async_copy(zero_ref.at[pl.ds(0, size)], xs_ref.at[pl.ds(off, size)], zsem)

                    @pl.when(hit)
                    def _(cp=cp):
                        cp.start() if start else cp.wait()

                    off = off + jnp.where(hit, size, 0)
                    size //= 2
                return carry
            lax.fori_loop(0, n_blocks, body, 0)

        fill(True)
        fill(False)

    def copy(t, kk):
        return pltpu.make_async_copy(hf_ref.at[t], xs_ref.at[dest_ref[0, 0, t * TOP_K + kk]], sem)

    def issue(t, carry):
        for kk in range(TOP_K):
            copy(t, kk).start(priority=kk % 2)
        return carry

    def drain(t, carry):
        for kk in range(TOP_K):
            copy(t, kk).wait()
        return carry

    lax.fori_loop(0, tt, issue, 0)
    lax.fori_loop(0, tt, drain, 0)


def _dispatch(dest2, nr, hf3, n_blocks, tt):
    t = hf3.shape[0]
    return pl.pallas_call(
        functools.partial(_dispatch_kernel, tt=tt, n_blocks=n_blocks),
        grid=(t // tt,),
        in_specs=[
            pl.BlockSpec((1, 1, tt * TOP_K), lambda i: (i, 0, 0), memory_space=pltpu.SMEM),
            pl.BlockSpec(memory_space=pltpu.SMEM),
            pl.BlockSpec((tt, ROW_SUB, LANES), lambda i: (i, 0, 0)),
        ],
        out_specs=pl.BlockSpec(memory_space=pl.ANY),
        out_shape=jax.ShapeDtypeStruct((n_blocks * MOE_TM, ROW_SUB, LANES), BF16),
        scratch_shapes=[pltpu.VMEM((MOE_TM, ROW_SUB, LANES), BF16), pltpu.SemaphoreType.DMA(()),
                        pltpu.SemaphoreType.DMA(())],
        compiler_params=_cparams(("arbitrary",)),
        name="dispatch",
    )(dest2, nr, hf3)


def _stream_expert_weights(meta_refs, i, copies):
    be_ref, _, _, first_ref, seg_ref, nxt_ref, nseg_ref = meta_refs
    k = seg_ref[i]
    slot = lax.rem(k, 2)

    @pl.when(first_ref[i] == 1)
    def _():
        @pl.when(k == 0)
        def _():
            for c in copies(be_ref[i], slot):
                c.start()

        @pl.when(k < nseg_ref[0] - 1)
        def _():
            for c in copies(nxt_ref[i], 1 - slot):
                c.start()

        for c in copies(be_ref[i], slot):
            c.wait()

    return slot


def _dot_w(x, w):
    return lax.dot_general(x, w, (((1,), (0,)), ((), ())), preferred_element_type=F32)


def _moe_up_kernel(be_ref, bx_ref, nr_ref, first_ref, seg_ref, nxt_ref, nseg_ref, x_ref, w_hbm, bg_ref,
                   bu_ref, o_ref, wbuf_ref, sem):
    f = pl.program_id(0)
    i = pl.program_id(1)
    nrows = nr_ref[i]
    nf = D_FF // MOE_TF

    def copies(e, slot):
        out = []
        for part in range(2):
            col = pl.multiple_of((part * nf + f) * MOE_TF, MOE_TF)
            out.append(pltpu.make_async_copy(w_hbm.at[e, :, pl.ds(col, MOE_TF)], wbuf_ref.at[slot, part],
                                             sem.at[slot, part]))
        return out

    slot = _stream_expert_weights((be_ref, bx_ref, nr_ref, first_ref, seg_ref, nxt_ref, nseg_ref), i, copies)

    for h in range(MOE_TM // MOE_SUB):
        rows = slice(h * MOE_SUB, (h + 1) * MOE_SUB)

        @pl.when(nrows > h * MOE_SUB)
        def _(rows=rows):
            xb = x_ref[rows].reshape(MOE_SUB, D_MODEL)
            g = jnp.minimum(_dot_w(xb, wbuf_ref[slot, 0]) + bg_ref[0], SWIGLU_LIMIT)
            u = jnp.clip(_dot_w(xb, wbuf_ref[slot, 1]) + bu_ref[0], -SWIGLU_LIMIT, SWIGLU_LIMIT)
            act = g * jax.nn.sigmoid(SWIGLU_ALPHA * g) * (u + 1.0)
            o_ref[rows, :] = act.astype(o_ref.dtype)

        @pl.when(nrows <= h * MOE_SUB)
        def _(rows=rows):
            o_ref[rows, :] = jnp.zeros((MOE_SUB, MOE_TF), o_ref.dtype)


def _moe_up(meta, xs3, w1, b1, n_blocks):
    n_slots = xs3.shape[0]
    nf = D_FF // MOE_TF
    grid_spec = pltpu.PrefetchScalarGridSpec(
        num_scalar_prefetch=len(meta),
        grid=(nf, n_blocks),
        in_specs=[
            pl.BlockSpec((MOE_TM, ROW_SUB, LANES), lambda f, i, be, bx, *_: (bx[i], 0, 0)),
            pl.BlockSpec(memory_space=pl.ANY),
            pl.BlockSpec((1, 1, MOE_TF), lambda f, i, be, bx, *_: (be[i], 0, f)),
            pl.BlockSpec((1, 1, MOE_TF), lambda f, i, be, bx, *_: (be[i], 0, nf + f)),
        ],
        out_specs=pl.BlockSpec((MOE_TM, MOE_TF), lambda f, i, *_: (i, f)),
        scratch_shapes=[
            pltpu.VMEM((2, 2, D_MODEL, MOE_TF), F32),
            pltpu.SemaphoreType.DMA((2, 2)),
        ],
    )
    return pl.pallas_call(
        _moe_up_kernel,
        grid_spec=grid_spec,
        out_shape=jax.ShapeDtypeStruct((n_slots, D_FF), BF16),
        compiler_params=_cparams(("arbitrary", "arbitrary")),
        name="moe_up",
    )(*meta, xs3, w1, b1, b1)


def _moe_down_kernel(be_ref, bx_ref, nr_ref, first_ref, seg_ref, nxt_ref, nseg_ref, a_ref, w_hbm, b_ref,
                     o_ref, wbuf_ref, sem):
    i = pl.program_id(0)
    nrows = nr_ref[i]

    def copies(e, slot):
        return [pltpu.make_async_copy(w_hbm.at[e], wbuf_ref.at[slot], sem.at[slot])]

    slot = _stream_expert_weights((be_ref, bx_ref, nr_ref, first_ref, seg_ref, nxt_ref, nseg_ref), i, copies)

    for h in range(MOE_TM // MOE_SUB):
        rows = slice(h * MOE_SUB, (h + 1) * MOE_SUB)

        @pl.when(nrows > h * MOE_SUB)
        def _(rows=rows):
            y = _dot_w(a_ref[rows, :], wbuf_ref[slot]) + b_ref[0]
            o_ref[rows] = y.astype(o_ref.dtype).reshape(MOE_SUB, ROW_SUB, LANES)

        @pl.when(nrows <= h * MOE_SUB)
        def _(rows=rows):
            o_ref[rows] = jnp.zeros((MOE_SUB, ROW_SUB, LANES), o_ref.dtype)


def _moe_down(meta, act, w2, b2, n_blocks):
    n_slots = act.shape[0]
    grid_spec = pltpu.PrefetchScalarGridSpec(
        num_scalar_prefetch=len(meta),
        grid=(n_blocks,),
        in_specs=[
            pl.BlockSpec((MOE_TM, D_FF), lambda i, be, bx, *_: (bx[i], 0)),
            pl.BlockSpec(memory_space=pl.ANY),
            pl.BlockSpec((1, 1, D_MODEL), lambda i, be, bx, *_: (be[i], 0, 0)),
        ],
        out_specs=pl.BlockSpec((MOE_TM, ROW_SUB, LANES), lambda i, *_: (i, 0, 0)),
        scratch_shapes=[
            pltpu.VMEM((2, D_FF, D_MODEL), F32),
            pltpu.SemaphoreType.DMA((2,)),
        ],
    )
    return pl.pallas_call(
        _moe_down_kernel,
        grid_spec=grid_spec,
        out_shape=jax.ShapeDtypeStruct((n_slots, ROW_SUB, LANES), BF16),
        compiler_params=_cparams(("arbitrary",)),
        name="moe_down",
    )(*meta, act, w2, b2)


def _combine_kernel(dest_ref, dnext_ref, y_ref, gate_ref, h_ref, o_ref, buf_ref, sem, *, tc):
    i = pl.program_id(0)
    slot = lax.rem(i, 2)

    def copy(d_ref, s, t, kk):
        return pltpu.make_async_copy(y_ref.at[d_ref[0, 0, t * TOP_K + kk]], buf_ref.at[s, kk * tc + t], sem.at[s])

    def issue(d_ref, s):
        def body(t, carry):
            for kk in range(TOP_K):
                copy(d_ref, s, t, kk).start(priority=kk % 2)
            return carry
        lax.fori_loop(0, tc, body, 0)

    @pl.when(i == 0)
    def _():
        issue(dest_ref, slot)

    @pl.when(i + 1 < pl.num_programs(0))
    def _():
        issue(dnext_ref, 1 - slot)

    def drain(t, carry):
        for kk in range(TOP_K):
            copy(dest_ref, slot, t, kk).wait()
        return carry

    lax.fori_loop(0, tc, drain, 0)

    gates = gate_ref[...]
    acc = h_ref[...]
    for kk in range(TOP_K):
        yk = buf_ref[slot, kk * tc:(kk + 1) * tc].reshape(tc, ROW_SUB * LANES).astype(F32)
        acc = acc + gates[:, kk:kk + 1] * yk
    o_ref[...] = acc


def _combine(dest2, y3, gates, h2, tc):
    t, d = h2.shape
    nt = t // tc
    return pl.pallas_call(
        functools.partial(_combine_kernel, tc=tc),
        grid=(nt,),
        in_specs=[
            pl.BlockSpec((1, 1, tc * TOP_K), lambda i: (i, 0, 0), memory_space=pltpu.SMEM),
            pl.BlockSpec((1, 1, tc * TOP_K), lambda i: (jnp.minimum(i + 1, nt - 1), 0, 0), memory_space=pltpu.SMEM),
            pl.BlockSpec(memory_space=pl.ANY),
            pl.BlockSpec((tc, LANES), lambda i: (i, 0)),
            pl.BlockSpec((tc, d), lambda i: (i, 0)),
        ],
        out_specs=pl.BlockSpec((tc, d), lambda i: (i, 0)),
        out_shape=jax.ShapeDtypeStruct((t, d), F32),
        scratch_shapes=[pltpu.VMEM((2, TOP_K * tc, ROW_SUB, LANES), BF16), pltpu.SemaphoreType.DMA((2,))],
        compiler_params=_cparams(("arbitrary",)),
        name="combine",
    )(dest2, dest2, y3, gates, h2)


def _moe_plan(counts, route, n_blocks):
    idx = route[:, :TOP_K].reshape(-1)
    rank = route[:, TOP_K:2 * TOP_K].reshape(-1)
    nblk = (counts + MOE_TM - 1) // MOE_TM
    blk_end = jnp.cumsum(nblk)
    blk_start = blk_end - nblk
    dest = (blk_start * MOE_TM)[idx] + rank
    total = blk_end[-1]
    ids = jnp.arange(n_blocks, dtype=I32)
    live = ids < total
    src = jnp.minimum(ids, total - 1)
    be = jnp.minimum(jnp.sum(blk_end[None, :] <= src[:, None], axis=1), N_EXPERTS - 1).astype(I32)
    nr = jnp.where(live, jnp.clip(counts[be] - (ids - blk_start[be]) * MOE_TM, 0, MOE_TM), 0).astype(I32)
    first = jnp.logical_and(live, ids == blk_start[be]).astype(I32)
    has = counts > 0
    ordinal = jnp.cumsum(has.astype(I32)) - 1
    e_ids = jnp.arange(N_EXPERTS, dtype=I32)
    later = jnp.logical_and(has[None, :], e_ids[None, :] > e_ids[:, None])
    nxt_e = jnp.min(jnp.where(later, e_ids[None, :], N_EXPERTS), axis=1)
    nxt_e = jnp.where(nxt_e == N_EXPERTS, jnp.argmax(has).astype(I32), nxt_e)
    nseg = jnp.sum(has.astype(I32)).reshape(1)
    return dest.astype(I32), (be, src.astype(I32), nr, first, ordinal[be].astype(I32), nxt_e[be].astype(I32), nseg)


def _layer(h, mem, lam_init, norm_mix_g, w_in, conv_w, conv_b, dt_bias, a_log, d_skip, ssd_norm_g,
           da_q_norm_g, da_k_norm_g, lq1, lk1, lq2, lk2, da_subln_g, w_out, norm_xa_g, norm_mem_g,
           xa_wq, xa_wkv, xa_q_norm_g, xa_k_norm_g, xa_wo, norm_ffn_g, router_w, router_b,
           moe_w1, moe_b1, moe_w2, moe_b2):
    bsz, seq, d = h.shape
    t = bsz * seq
    mem_len = mem.shape[1]
    row = lambda v: v.astype(F32).reshape(1, -1)
    x2 = h.reshape(t, d)

    o1 = SSD_WIDTH + XBC_WIDTH
    o2 = o1 + SSD_HEADS
    w_main = _w_prep(w_in)
    w_dt = _split3_host(jnp.pad(w_in[:, o1:o2], ((0, 0), (0, LANES - SSD_HEADS))))
    proj, dt_raw = _in_proj(x2, row(norm_mix_g), w_main, w_dt)
    proj3 = proj.reshape(bsz, seq, PROJ_COLS)
    y_ssd = _ssd(proj3, dt_raw.reshape(bsz, seq, LANES), conv_w, conv_b, dt_bias, a_log, d_skip, ssd_norm_g)

    gq = row(jnp.tile(da_q_norm_g.astype(F32), 2) * (DA_HALF ** -0.5))
    gk = row(jnp.tile(da_k_norm_g.astype(F32), 2))
    lvec = jnp.stack([lq1, lk1, lq2, lk2]).astype(F32)
    y_da = _diff_attn(proj3, gq, gk, lvec, row(da_subln_g), lam_init)

    h1 = _mm_res([y_ssd.reshape(t, SSD_WIDTH), y_da.reshape(t, DA_WIDTH)], w_out.astype(BF16), x2)

    eq = jnp.tile(xa_q_norm_g.astype(F32) * (XA_HEAD_DIM ** -0.5), XA_HEADS).reshape(1, -1)
    qx = _norm_proj(h1, row(norm_xa_g), xa_wq.astype(BF16), eq[:, :XA_HEAD_DIM], XA_HEADS, 1024, XA_HEAD_DIM)
    kv = _norm_proj(mem.reshape(bsz * mem_len, d), row(norm_mem_g), xa_wkv.astype(BF16),
                    row(xa_k_norm_g), XA_HEADS, bsz * mem_len, XA_HEAD_DIM)
    ox = _xattn(qx, kv, seq, mem_len)
    h2 = _mm_res([ox], xa_wo.astype(BF16), h1)

    rw3 = _split3_host(jnp.pad(router_w, ((0, 0), (0, LANES - N_EXPERTS))))
    rb = jnp.pad(router_b.astype(F32), (0, LANES - N_EXPERTS), constant_values=-jnp.inf).reshape(1, LANES)
    hf, route, gates, cnt = _router(h2, row(norm_ffn_g), rw3, rb)
    n_blocks = (t * TOP_K) // MOE_TM + N_EXPERTS
    n_slots = n_blocks * MOE_TM
    dest, meta = _moe_plan(cnt[0, :N_EXPERTS].astype(I32), route, n_blocks)
    tt = 512
    xs = _dispatch(dest.reshape(t // tt, 1, tt * TOP_K), meta[2], hf, n_blocks, tt)
    act = _moe_up(meta, xs, moe_w1,
                  moe_b1.reshape(N_EXPERTS, 1, 2 * D_FF), n_blocks)
    y = _moe_down(meta, act, moe_w2, moe_b2.reshape(N_EXPERTS, 1, D_MODEL), n_blocks)
    tc = 256
    out = _combine(dest.reshape(t // tc, 1, tc * TOP_K), y, gates, h2, tc)
    return out.reshape(bsz, seq, d)


def kernel(x, mem, norm_mix_g, w_in, conv_w, conv_b, dt_bias, a_log, d_skip, ssd_norm_g, da_q_norm_g,
           da_k_norm_g, lambda_q1, lambda_k1, lambda_q2, lambda_k2, da_subln_g, w_out, norm_xa_g,
           norm_mem_g, xa_wq, xa_wkv, xa_q_norm_g, xa_k_norm_g, xa_wo, norm_ffn_g, router_w, router_b,
           moe_w1, moe_b1, moe_w2, moe_b2):
    h = x
    for layer in range(norm_mix_g.shape[0]):
        lam_init = 0.8 - 0.6 * math.exp(-0.3 * layer)
        h = _layer(h, mem, lam_init, norm_mix_g[layer], w_in[layer], conv_w[layer], conv_b[layer],
                   dt_bias[layer], a_log[layer], d_skip[layer], ssd_norm_g[layer], da_q_norm_g[layer],
                   da_k_norm_g[layer], lambda_q1[layer], lambda_k1[layer], lambda_q2[layer],
                   lambda_k2[layer], da_subln_g[layer], w_out[layer], norm_xa_g[layer], norm_mem_g[layer],
                   xa_wq[layer], xa_wkv[layer], xa_q_norm_g[layer], xa_k_norm_g[layer], xa_wo[layer],
                   norm_ffn_g[layer], router_w[layer], router_b[layer], moe_w1[layer], moe_b1[layer],
                   moe_w2[layer], moe_b2[layer])
    return h
```

```python
import functools
import math

import jax
import jax.numpy as jnp
from jax import lax
from jax.experimental import pallas as pl
from jax.experimental.pallas import tpu as pltpu

F32 = jnp.float32
BF16 = jnp.bfloat16
I32 = jnp.int32

D_MODEL = 2048
SSD_WIDTH = 1024
SSD_HEAD_DIM = 64
SSD_HEADS = 16
SSD_GROUPS = 2
SSD_STATE = 128
SSD_CONV = 4
SSD_CHUNK = 128
XBC_WIDTH = SSD_WIDTH + 2 * SSD_GROUPS * SSD_STATE
DA_WIDTH = 1024
DA_HEAD_DIM = 128
DA_HALF = 64
DA_HEADS = 8
XA_HEADS = 4
XA_HEAD_DIM = D_MODEL // XA_HEADS
N_EXPERTS = 32
TOP_K = 4
D_FF = 2048
SWIGLU_LIMIT = 7.0
SWIGLU_ALPHA = 1.702
EPS = 1e-6

LANES = 128
ROW_SUB = D_MODEL // LANES
VMEM_LIMIT = 56 * 1024 * 1024

PROJ_COLS = SSD_WIDTH + XBC_WIDTH + 3 * DA_WIDTH
COL_Q = (SSD_WIDTH + XBC_WIDTH) // LANES
COL_V = COL_Q + 2 * DA_HEADS

MOE_TM = 512
MOE_SUB = 256
MOE_TF = 1024
COMBINE_CHUNK = 8


def _dot(a, b):
    return jnp.dot(a, b, preferred_element_type=F32)


def _dot_nt(a, b):
    return lax.dot_general(a, b, (((1,), (1,)), ((), ())), preferred_element_type=F32)


def _dot_tn(a, b):
    return lax.dot_general(a, b, (((0,), (0,)), ((), ())), preferred_element_type=F32)


def _rms(x):
    return x * lax.rsqrt(jnp.mean(x * x, axis=-1, keepdims=True) + EPS)


def _split2(x):
    hi = x.astype(BF16)
    mid = (x - hi.astype(F32)).astype(BF16)
    return hi, mid


def _split3(x):
    hi = x.astype(BF16)
    r = x - hi.astype(F32)
    mid = r.astype(BF16)
    lo = (r - mid.astype(F32)).astype(BF16)
    return hi, mid, lo


def _split3_host(w):
    hi, mid, lo = _split3(w.astype(F32))
    return jnp.stack([hi, mid, lo])


def _silu(x):
    return x * jax.nn.sigmoid(x)


def _cparams(sem):
    return pltpu.CompilerParams(dimension_semantics=sem, vmem_limit_bytes=VMEM_LIMIT)


def _w_prep_kernel(w_ref, o_ref):
    x = w_ref[...]
    o1 = SSD_WIDTH + XBC_WIDTH
    o_ref[...] = jnp.concatenate([x[:, :o1], x[:, o1 + SSD_HEADS:]], axis=1).astype(o_ref.dtype)


def _w_prep(w_in, tr=256):
    k, n = w_in.shape
    return pl.pallas_call(
        _w_prep_kernel,
        grid=(k // tr,),
        in_specs=[pl.BlockSpec((tr, n), lambda i: (i, 0))],
        out_specs=pl.BlockSpec((tr, PROJ_COLS), lambda i: (i, 0)),
        out_shape=jax.ShapeDtypeStruct((k, PROJ_COLS), BF16),
        compiler_params=_cparams(("parallel",)),
        name="w_prep",
    )(w_in)


def _in_proj_kernel(x_ref, g_ref, w_ref, wdt_ref, o_ref, dt_ref, xn_ref):
    @pl.when(pl.program_id(1) == 0)
    def _():
        hn = _rms(x_ref[...]) * g_ref[...]
        hi, mid = _split2(hn)
        xn_ref[...] = hi
        dt_ref[...] = _dot(hi, wdt_ref[0]) + (_dot(hi, wdt_ref[1]) + _dot(mid, wdt_ref[0]))

    o_ref[...] = _dot(xn_ref[...], w_ref[...]).astype(o_ref.dtype)


def _in_proj(x2, g, w, wdt3, tm=1024, tn=512):
    m, k = x2.shape
    n = w.shape[1]
    return pl.pallas_call(
        _in_proj_kernel,
        grid=(m // tm, n // tn),
        in_specs=[
            pl.BlockSpec((tm, k), lambda i, j: (i, 0)),
            pl.BlockSpec((1, k), lambda i, j: (0, 0)),
            pl.BlockSpec((k, tn), lambda i, j: (0, j)),
            pl.BlockSpec((3, k, LANES), lambda i, j: (0, 0, 0)),
        ],
        out_specs=[
            pl.BlockSpec((tm, tn), lambda i, j: (i, j)),
            pl.BlockSpec((tm, LANES), lambda i, j: (i, 0)),
        ],
        out_shape=[jax.ShapeDtypeStruct((m, n), BF16), jax.ShapeDtypeStruct((m, LANES), F32)],
        scratch_shapes=[pltpu.VMEM((tm, k), BF16)],
        compiler_params=_cparams(("parallel", "arbitrary")),
        name="in_proj",
    )(x2, g, w, wdt3)


def _norm_proj_kernel(x_ref, g_ref, w_ref, eg_ref, o_ref, xn_ref, *, epi_tiles):
    j = pl.program_id(1)

    @pl.when(j == 0)
    def _():
        xn_ref[...] = (_rms(x_ref[...]) * g_ref[...]).astype(BF16)

    acc = _dot(xn_ref[...], w_ref[...])

    @pl.when(j < epi_tiles)
    def _():
        o_ref[...] = (_rms(acc) * eg_ref[...]).astype(o_ref.dtype)

    @pl.when(j >= epi_tiles)
    def _():
        o_ref[...] = acc.astype(o_ref.dtype)


def _norm_proj(x2, g, w, eg, epi_tiles, tm, tn):
    m, k = x2.shape
    n = w.shape[1]
    return pl.pallas_call(
        functools.partial(_norm_proj_kernel, epi_tiles=epi_tiles),
        grid=(m // tm, n // tn),
        in_specs=[
            pl.BlockSpec((tm, k), lambda i, j: (i, 0)),
            pl.BlockSpec((1, k), lambda i, j: (0, 0)),
            pl.BlockSpec((k, tn), lambda i, j: (0, j)),
            pl.BlockSpec((1, tn), lambda i, j: (0, 0)),
        ],
        out_specs=pl.BlockSpec((tm, tn), lambda i, j: (i, j)),
        out_shape=jax.ShapeDtypeStruct((m, n), BF16),
        scratch_shapes=[pltpu.VMEM((tm, k), BF16)],
        compiler_params=_cparams(("parallel", "arbitrary")),
        name="norm_proj",
    )(x2, g, w, eg)


def _mm_res_kernel(*refs, n_a):
    a_refs, w_refs = refs[:n_a], refs[n_a:2 * n_a]
    res_ref, o_ref = refs[2 * n_a], refs[2 * n_a + 1]
    acc = res_ref[...]
    for a, w in zip(a_refs, w_refs):
        acc = acc + _dot(a[...], w[...])
    o_ref[...] = acc


def _mm_res(a_list, w, res, tm=1024, tn=1024):
    m, n = res.shape
    n_a = len(a_list)
    in_specs = []
    for a in a_list:
        in_specs.append(pl.BlockSpec((tm, a.shape[1]), lambda i, j: (i, 0)))
    for idx, a in enumerate(a_list):
        in_specs.append(pl.BlockSpec((a.shape[1], tn), lambda i, j, idx=idx: (idx, j)))
    in_specs.append(pl.BlockSpec((tm, tn), lambda i, j: (i, j)))
    return pl.pallas_call(
        functools.partial(_mm_res_kernel, n_a=n_a),
        grid=(m // tm, n // tn),
        in_specs=in_specs,
        out_specs=pl.BlockSpec((tm, tn), lambda i, j: (i, j)),
        out_shape=jax.ShapeDtypeStruct((m, n), F32),
        compiler_params=_cparams(("parallel", "arbitrary")),
        name="mm_res",
    )(*a_list, *([w] * n_a), res)


def _ssd_kernel(z_ref, xs_ref, b_ref, c_ref, dtr_ref, cw_ref, cb_ref, dtb_ref, alog_ref,
                dsk_ref, ng_ref, sel_ref, o_ref, ext_ref, ht_ref, y_ref, *, nb):
    L = SSD_CHUNK

    @pl.when(pl.program_id(1) == 0)
    def _():
        ht_ref[...] = jnp.zeros_like(ht_ref)
        ext_ref[:, 0:8, :] = jnp.zeros((nb, 8, XBC_WIDTH), F32)

    row = lax.broadcasted_iota(I32, (L, L), 0)
    col = lax.broadcasted_iota(I32, (L, L), 1)
    causal = col <= row
    tril = jnp.where(causal, 1.0, 0.0).astype(BF16)
    lane = lax.broadcasted_iota(I32, (L, LANES), 1)
    first = lane < SSD_HEAD_DIM
    a = -jnp.exp(alog_ref[...])

    for bb in range(nb):
        _ssd_chunk(bb, z_ref, xs_ref, b_ref, c_ref, dtr_ref, cw_ref, cb_ref, dtb_ref, a, dsk_ref, ng_ref,
                   sel_ref, o_ref, ext_ref, ht_ref, y_ref, causal, tril, first)


def _ssd_chunk(bb, z_ref, xs_ref, b_ref, c_ref, dtr_ref, cw_ref, cb_ref, dtb_ref, a, dsk_ref, ng_ref,
               sel_ref, o_ref, ext_ref, ht_ref, y_ref, causal, tril, first):
    L = SSD_CHUNK
    n_pairs = SSD_HEADS // 2
    pairs_per_group = n_pairs // SSD_GROUPS
    c_off = SSD_WIDTH + SSD_GROUPS * SSD_STATE
    ext = ext_ref.at[bb]
    ext[8:8 + L, 0:SSD_WIDTH] = xs_ref[bb].astype(F32)
    ext[8:8 + L, SSD_WIDTH:c_off] = b_ref[bb].astype(F32)
    ext[8:8 + L, c_off:XBC_WIDTH] = c_ref[bb].astype(F32)
    acc = jnp.broadcast_to(cb_ref[...], (L, XBC_WIDTH))
    for k in range(SSD_CONV):
        acc = acc + cw_ref[k:k + 1, :] * ext[pl.ds(8 - (SSD_CONV - 1) + k, L), :]
    ext[0:8, :] = ext[L:L + 8, :]
    xbc = _silu(acc)

    dt = jax.nn.softplus(dtr_ref[bb] + dtb_ref[...])
    adt = dt * a
    a_hi, a_mid, a_lo = _split3(adt)
    acum = _dot(tril, a_hi) + (_dot(tril, a_mid) + _dot(tril, a_lo))
    acum_t = acum.T
    pieces = [jnp.concatenate([p, q], axis=0) for p, q in zip(_split3(acum), _split3(dt))]

    cbs = []
    for g in range(SSD_GROUPS):
        bg = xbc[:, SSD_WIDTH + g * SSD_STATE:SSD_WIDTH + (g + 1) * SSD_STATE].astype(BF16)
        cg = xbc[:, c_off + g * SSD_STATE:c_off + (g + 1) * SSD_STATE].astype(BF16)
        cbs.append((bg, cg, _dot_nt(cg, bg)))

    for p in range(n_pairs):
        bg, cg, cb = cbs[p // pairs_per_group]
        sel = sel_ref[:, 2 * LANES * p:2 * LANES * (p + 1)]
        fb = _dot(pieces[0], sel) + (_dot(pieces[1], sel) + _dot(pieces[2], sel))
        ms = []
        for i in range(2):
            h = 2 * p + i
            seg = fb[:L, i * LANES:(i + 1) * LANES] - acum_t[h:h + 1, :]
            dec = jnp.exp(jnp.where(causal, seg, -jnp.inf))
            ms.append((cb * dec).astype(BF16))
        ac = jnp.where(first, fb[:L, :LANES], fb[:L, LANES:])
        dtp = jnp.where(first, fb[L:, :LANES], fb[L:, LANES:])
        xp = xbc[:, p * LANES:(p + 1) * LANES]
        xdt = xp * dtp
        y = _dot(ms[0], jnp.where(first, xdt, 0.0).astype(BF16))
        y = y + _dot(ms[1], jnp.where(first, 0.0, xdt).astype(BF16))
        ht = ht_ref[bb * n_pairs + p]
        y = y + _dot(cg, ht.astype(BF16)) * jnp.exp(ac)
        a_last = ac[L - 1:L, :]
        ht_ref[bb * n_pairs + p] = (ht * jnp.exp(a_last)
                                    + _dot_tn(bg, (xdt * jnp.exp(a_last - ac)).astype(BF16)))
        y_ref[bb, :, p * LANES:(p + 1) * LANES] = y + xp * dsk_ref[:, p * LANES:(p + 1) * LANES]

    yg = y_ref[bb] * _silu(z_ref[bb].astype(F32))
    gw = SSD_WIDTH // SSD_GROUPS
    for g in range(SSD_GROUPS):
        blk = _rms(yg[:, g * gw:(g + 1) * gw]) * ng_ref[:, g * gw:(g + 1) * gw]
        o_ref[bb, :, g * gw:(g + 1) * gw] = blk.astype(o_ref.dtype)


def _ssd(proj3, dt3, conv_w, conv_b, dt_bias, a_log, d_skip, norm_g, nb=2):
    bsz, s, _ = proj3.shape
    L = SSD_CHUNK
    pad = LANES - SSD_HEADS
    dtb = jnp.pad(dt_bias.astype(F32), (0, pad)).reshape(1, LANES)
    alog = jnp.pad(a_log.astype(F32), (0, pad)).reshape(1, LANES)
    dsk = jnp.repeat(d_skip.astype(F32), SSD_HEAD_DIM).reshape(1, SSD_WIDTH)
    sel = (jnp.arange(LANES)[:, None] == (jnp.arange(SSD_HEADS * LANES)[None, :] // LANES)).astype(BF16)
    const = lambda shape: pl.BlockSpec(shape, lambda b, c: (0,) * len(shape))
    return pl.pallas_call(
        functools.partial(_ssd_kernel, nb=nb),
        grid=(bsz // nb, s // L),
        in_specs=[
            pl.BlockSpec((nb, L, SSD_WIDTH), lambda b, c: (b, c, 0)),
            pl.BlockSpec((nb, L, SSD_WIDTH), lambda b, c: (b, c, 1)),
            pl.BlockSpec((nb, L, 2 * SSD_STATE), lambda b, c: (b, c, 2 * SSD_WIDTH // (2 * SSD_STATE))),
            pl.BlockSpec((nb, L, 2 * SSD_STATE), lambda b, c: (b, c, 2 * SSD_WIDTH // (2 * SSD_STATE) + 1)),
            pl.BlockSpec((nb, L, LANES), lambda b, c: (b, c, 0)),
            const((SSD_CONV, XBC_WIDTH)), const((1, XBC_WIDTH)), const((1, LANES)), const((1, LANES)),
            const((1, SSD_WIDTH)), const((1, SSD_WIDTH)), const((LANES, SSD_HEADS * LANES)),
        ],
        out_specs=pl.BlockSpec((nb, L, SSD_WIDTH), lambda b, c: (b, c, 0)),
        out_shape=jax.ShapeDtypeStruct((bsz, s, SSD_WIDTH), BF16),
        scratch_shapes=[
            pltpu.VMEM((nb, L + 8, XBC_WIDTH), F32),
            pltpu.VMEM((nb * SSD_HEADS // 2, SSD_STATE, LANES), F32),
            pltpu.VMEM((nb, L, SSD_WIDTH), F32),
        ],
        compiler_params=_cparams(("parallel", "arbitrary")),
        name="ssd",
    )(proj3, proj3, proj3, proj3, dt3, conv_w.astype(F32), conv_b.astype(F32).reshape(1, XBC_WIDTH),
      dtb, alog, dsk, norm_g.astype(F32).reshape(1, SSD_WIDTH), sel)


def _half_norm(x, g):
    ra = lax.broadcasted_iota(I32, (LANES, LANES), 0) // DA_HALF
    rb = lax.broadcasted_iota(I32, (LANES, LANES), 1) // DA_HALF
    grp = jnp.where(ra == rb, 1.0, 0.0).astype(BF16)
    hi, mid = _split2(x * x)
    ss = _dot(hi, grp) + _dot(mid, grp)
    return x * lax.rsqrt(ss * (1.0 / DA_HALF) + EPS) * g


def _da_kernel(q_ref, k_ref, v_ref, gq_ref, gk_ref, lv_ref, sg_ref, o_ref,
               kn_ref, va_ref, q2_ref, m_ref, l_ref, acc_ref, *, lam_init, tq, seq):
    qi = pl.program_id(2)
    tk = tq

    @pl.when(qi == 0)
    def _():
        for c in range(seq // tk):
            rows = slice(c * tk, (c + 1) * tk)
            kn_ref[rows, :] = _half_norm(k_ref[0, rows, :].astype(F32), gk_ref[...]).astype(BF16)
            va_ref[rows, :LANES] = v_ref[0, rows, :]
            va_ref[rows, LANES:] = jnp.ones((tk, LANES), BF16)

    q = _half_norm(q_ref[0].astype(F32), gq_ref[...]).astype(BF16)
    first = lax.broadcasted_iota(I32, q.shape, 1) < DA_HALF
    zero = jnp.zeros_like(q)
    q2_ref[0] = jnp.where(first, q, zero)
    q2_ref[1] = jnp.where(first, zero, q)
    m_ref[...] = jnp.full(m_ref.shape, -jnp.inf, F32)
    l_ref[...] = jnp.zeros_like(l_ref)
    acc_ref[...] = jnp.zeros_like(acc_ref)

    def step(c, masked):
        k = kn_ref[c * tk:(c + 1) * tk, :]
        va = va_ref[c * tk:(c + 1) * tk, :]
        if masked:
            keep = (lax.broadcasted_iota(I32, (tq, tk), 1) <= lax.broadcasted_iota(I32, (tq, tk), 0))
        for j in range(2):
            s = _dot_nt(q2_ref[j], k)
            if masked:
                s = jnp.where(keep, s, -jnp.inf)
            m_prev = m_ref[j]
            m_new = jnp.maximum(m_prev, jnp.max(s, axis=-1, keepdims=True))
            alpha = jnp.exp(m_prev - m_new)
            p = jnp.exp(s - jnp.concatenate([m_new] * (tk // LANES), axis=1))
            pv = _dot(p.astype(BF16), va)
            l_ref[j] = alpha * l_ref[j] + pv[:, LANES:]
            acc_ref[j] = alpha * acc_ref[j] + pv[:, :LANES]
            m_ref[j] = m_new

    for qv in range(seq // tq):
        @pl.when(qi == qv)
        def _(qv=qv):
            for c in range(qv):
                step(c, False)
            step(qv, True)

    lv = lv_ref[...]
    lam = (jnp.exp(jnp.sum(lv[0:1] * lv[1:2], axis=-1, keepdims=True))
           - jnp.exp(jnp.sum(lv[2:3] * lv[3:4], axis=-1, keepdims=True)) + lam_init)
    o = acc_ref[0] / l_ref[0] - lam * (acc_ref[1] / l_ref[1])
    o_ref[0] = (_rms(o) * (sg_ref[...] * (1.0 - lam_init))).astype(o_ref.dtype)


def _diff_attn(proj3, gq, gk, lvec, subln_g, lam_init, tq=512):
    bsz, s, _ = proj3.shape
    const = lambda shape: pl.BlockSpec(shape, lambda b, h, qi: (0,) * len(shape))
    return pl.pallas_call(
        functools.partial(_da_kernel, lam_init=lam_init, tq=tq, seq=s),
        grid=(bsz, DA_HEADS, s // tq),
        in_specs=[
            pl.BlockSpec((1, tq, LANES), lambda b, h, qi: (b, qi, COL_Q + h)),
            pl.BlockSpec((1, s, LANES), lambda b, h, qi: (b, 0, COL_Q + DA_HEADS + h)),
            pl.BlockSpec((1, s, LANES), lambda b, h, qi: (b, 0, COL_V + h)),
            const((1, LANES)), const((1, LANES)), const((4, DA_HALF)), const((1, LANES)),
        ],
        out_specs=pl.BlockSpec((1, tq, LANES), lambda b, h, qi: (b, qi, h)),
        out_shape=jax.ShapeDtypeStruct((bsz, s, DA_WIDTH), BF16),
        scratch_shapes=[
            pltpu.VMEM((s, LANES), BF16),
            pltpu.VMEM((s, 2 * LANES), BF16),
            pltpu.VMEM((2, tq, LANES), BF16),
            pltpu.VMEM((2, tq, LANES), F32),
            pltpu.VMEM((2, tq, LANES), F32),
            pltpu.VMEM((2, tq, LANES), F32),
        ],
        compiler_params=_cparams(("parallel", "parallel", "arbitrary")),
        name="diff_attn",
    )(proj3, proj3, proj3, gq, gk, lvec, subln_g)


def _xattn_kernel(q_ref, k_ref, v_ref, o_ref):
    s = _dot_nt(q_ref[...], k_ref[...])
    p = jnp.exp(s - jnp.max(s, axis=-1, keepdims=True))
    o = _dot(p.astype(BF16), v_ref[...]) / jnp.sum(p, axis=-1, keepdims=True)
    o_ref[...] = o.astype(o_ref.dtype)


def _xattn(qx, kv, seq, mem_len, tm=512):
    m = qx.shape[0]
    hd = XA_HEAD_DIM
    return pl.pallas_call(
        _xattn_kernel,
        grid=(m // tm, XA_HEADS),
        in_specs=[
            pl.BlockSpec((tm, hd), lambda i, j: (i, j)),
            pl.BlockSpec((mem_len, hd), lambda i, j: ((i * tm) // seq, j)),
            pl.BlockSpec((mem_len, hd), lambda i, j: ((i * tm) // seq, XA_HEADS + j)),
        ],
        out_specs=pl.BlockSpec((tm, hd), lambda i, j: (i, j)),
        out_shape=jax.ShapeDtypeStruct((m, XA_HEADS * hd), BF16),
        compiler_params=_cparams(("parallel", "parallel")),
        name="xattn",
    )(qx, kv, kv)


def _router_kernel(x_ref, g_ref, rw_ref, rb_ref, hf_ref, route_ref, gate_ref, cnt_ref, carry_ref, *, tm):
    i = pl.program_id(0)

    @pl.when(i == 0)
    def _():
        carry_ref[...] = jnp.zeros_like(carry_ref)

    hn = _rms(x_ref[...]) * g_ref[...]
    hi, mid = _split2(hn)
    hf_ref[...] = hi.reshape(tm, ROW_SUB, LANES)

    logits = _dot(hi, rw_ref[0]) + (_dot(hi, rw_ref[1]) + _dot(mid, rw_ref[0])) + rb_ref[...]
    lane = lax.broadcasted_iota(I32, (tm, LANES), 1)
    lane_f = lane.astype(F32)
    idxs, vals = [], []
    work = logits
    for _ in range(TOP_K):
        m = jnp.max(work, axis=-1, keepdims=True)
        ik = jnp.min(jnp.where(work == m, lane_f, float(LANES)), axis=-1, keepdims=True)
        idxs.append(ik)
        vals.append(m)
        work = jnp.where(lane_f == ik, -jnp.inf, work)
    es = [jnp.exp(v - vals[0]) for v in vals]
    den = es[0] + es[1] + es[2] + es[3]

    onehot = jnp.zeros((tm, LANES), F32)
    for ik in idxs:
        onehot = onehot + jnp.where(lane_f == ik, 1.0, 0.0)
    r = lax.broadcasted_iota(I32, (tm, tm), 0)
    c = lax.broadcasted_iota(I32, (tm, tm), 1)
    strict = jnp.where(c < r, 1.0, 0.0).astype(BF16)
    before = carry_ref[0:1, :] + _dot(strict, onehot.astype(BF16))
    carry_ref[0:1, :] = carry_ref[0:1, :] + jnp.sum(onehot, axis=0, keepdims=True)
    cnt_ref[...] = jnp.broadcast_to(carry_ref[0:1, :], cnt_ref.shape)

    route = jnp.zeros((tm, LANES), F32)
    gates = jnp.zeros((tm, LANES), F32)
    for kk in range(TOP_K):
        rank = jnp.sum(jnp.where(lane_f == idxs[kk], before, 0.0), axis=-1, keepdims=True)
        route = jnp.where(lane == kk, idxs[kk], route)
        route = jnp.where(lane == TOP_K + kk, rank, route)
        gates = jnp.where(lane == kk, es[kk] / den, gates)
    route_ref[...] = route.astype(I32)
    gate_ref[...] = gates


def _router(h2, g, rw3, rb, tm=512):
    t, d = h2.shape
    return pl.pallas_call(
        functools.partial(_router_kernel, tm=tm),
        grid=(t // tm,),
        in_specs=[
            pl.BlockSpec((tm, d), lambda i: (i, 0)),
            pl.BlockSpec((1, d), lambda i: (0, 0)),
            pl.BlockSpec((3, d, LANES), lambda i: (0, 0, 0)),
            pl.BlockSpec((1, LANES), lambda i: (0, 0)),
        ],
        out_specs=[
            pl.BlockSpec((tm, ROW_SUB, LANES), lambda i: (i, 0, 0)),
            pl.BlockSpec((tm, LANES), lambda i: (i, 0)),
            pl.BlockSpec((tm, LANES), lambda i: (i, 0)),
            pl.BlockSpec((8, LANES), lambda i: (0, 0)),
        ],
        out_shape=[
            jax.ShapeDtypeStruct((t, ROW_SUB, LANES), BF16),
            jax.ShapeDtypeStruct((t, LANES), I32),
            jax.ShapeDtypeStruct((t, LANES), F32),
            jax.ShapeDtypeStruct((8, LANES), F32),
        ],
        scratch_shapes=[pltpu.VMEM((8, LANES), F32)],
        compiler_params=_cparams(("arbitrary",)),
        name="router",
    )(h2, g, rw3, rb)


def _dispatch_kernel(dest_ref, nr_ref, hf_ref, xs_ref, zero_ref, sem, zsem, *, tt, n_blocks):
    @pl.when(pl.program_id(0) == 0)
    def _():
        zero_ref[...] = jnp.zeros_like(zero_ref)

        def fill(start):
            def body(b, carry):
                pad = MOE_TM - nr_ref[b]
                off = b * MOE_TM + nr_ref[b]
                size = MOE_TM
                while size >= 1:
                    hit = (pad & size) != 0
                    cp = pltpu.make_async_copy(zero_ref.at[pl.ds(0, size)], xs_ref.at[pl.ds(off, size)], zsem)

                    @pl.when(hit)
                    def _(cp=cp):
                        cp.start() if start else cp.wait()

                    off = off + jnp.where(hit, size, 0)
                    size //= 2
                return carry
            lax.fori_loop(0, n_blocks, body, 0)

        fill(True)
        fill(False)

    def copy(t, kk):
        return pltpu.make_async_copy(hf_ref.at[t], xs_ref.at[dest_ref[0, 0, t * TOP_K + kk]], sem)

    def issue(t, carry):
        for kk in range(TOP_K):
            copy(t, kk).start(priority=kk % 2)
        return carry

    def drain(t, carry):
        for kk in range(TOP_K):
            copy(t, kk).wait()
        return carry

    lax.fori_loop(0, tt, issue, 0)
    lax.fori_loop(0, tt, drain, 0)


def _dispatch(dest2, nr, hf3, n_blocks, tt):
    t = hf3.shape[0]
    return pl.pallas_call(
        functools.partial(_dispatch_kernel, tt=tt, n_blocks=n_blocks),
        grid=(t // tt,),
        in_specs=[
            pl.BlockSpec((1, 1, tt * TOP_K), lambda i: (i, 0, 0), memory_space=pltpu.SMEM),
            pl.BlockSpec(memory_space=pltpu.SMEM),
            pl.BlockSpec((tt, ROW_SUB, LANES), lambda i: (i, 0, 0)),
        ],
        out_specs=pl.BlockSpec(memory_space=pl.ANY),
        out_shape=jax.ShapeDtypeStruct((n_blocks * MOE_TM, ROW_SUB, LANES), BF16),
        scratch_shapes=[pltpu.VMEM((MOE_TM, ROW_SUB, LANES), BF16), pltpu.SemaphoreType.DMA(()),
                        pltpu.SemaphoreType.DMA(())],
        compiler_params=_cparams(("arbitrary",)),
        name="dispatch",
    )(dest2, nr, hf3)


def _stream_expert_weights(meta_refs, i, copies):
    be_ref, _, _, first_ref, seg_ref, nxt_ref, nseg_ref = meta_refs
    k = seg_ref[i]
    slot = lax.rem(k, 2)

    @pl.when(first_ref[i] == 1)
    def _():
        @pl.when(k == 0)
        def _():
            for c in copies(be_ref[i], slot):
                c.start()

        @pl.when(k < nseg_ref[0] - 1)
        def _():
            for c in copies(nxt_ref[i], 1 - slot):
                c.start()

        for c in copies(be_ref[i], slot):
            c.wait()

    return slot


def _dot_w(x, w):
    return lax.dot_general(x, w, (((1,), (0,)), ((), ())), preferred_element_type=F32)


def _moe_up_kernel(be_ref, bx_ref, nr_ref, first_ref, seg_ref, nxt_ref, nseg_ref, x_ref, w_hbm, bg_ref,
                   bu_ref, o_ref, wbuf_ref, sem):
    f = pl.program_id(0)
    i = pl.program_id(1)
    nrows = nr_ref[i]
    nf = D_FF // MOE_TF

    def copies(e, slot):
        out = []
        for part in range(2):
            col = pl.multiple_of((part * nf + f) * MOE_TF, MOE_TF)
            out.append(pltpu.make_async_copy(w_hbm.at[e, :, pl.ds(col, MOE_TF)], wbuf_ref.at[slot, part],
                                             sem.at[slot, part]))
        return out

    slot = _stream_expert_weights((be_ref, bx_ref, nr_ref, first_ref, seg_ref, nxt_ref, nseg_ref), i, copies)

    for h in range(MOE_TM // MOE_SUB):
        rows = slice(h * MOE_SUB, (h + 1) * MOE_SUB)

        @pl.when(nrows > h * MOE_SUB)
        def _(rows=rows):
            xb = x_ref[rows].reshape(MOE_SUB, D_MODEL)
            g = jnp.minimum(_dot_w(xb, wbuf_ref[slot, 0]) + bg_ref[0], SWIGLU_LIMIT)
            u = jnp.clip(_dot_w(xb, wbuf_ref[slot, 1]) + bu_ref[0], -SWIGLU_LIMIT, SWIGLU_LIMIT)
            act = g * jax.nn.sigmoid(SWIGLU_ALPHA * g) * (u + 1.0)
            o_ref[rows, :] = act.astype(o_ref.dtype)

        @pl.when(nrows <= h * MOE_SUB)
        def _(rows=rows):
            o_ref[rows, :] = jnp.zeros((MOE_SUB, MOE_TF), o_ref.dtype)


def _moe_up(meta, xs3, w1, b1, n_blocks):
    n_slots = xs3.shape[0]
    nf = D_FF // MOE_TF
    grid_spec = pltpu.PrefetchScalarGridSpec(
        num_scalar_prefetch=len(meta),
        grid=(nf, n_blocks),
        in_specs=[
            pl.BlockSpec((MOE_TM, ROW_SUB, LANES), lambda f, i, be, bx, *_: (bx[i], 0, 0)),
            pl.BlockSpec(memory_space=pl.ANY),
            pl.BlockSpec((1, 1, MOE_TF), lambda f, i, be, bx, *_: (be[i], 0, f)),
            pl.BlockSpec((1, 1, MOE_TF), lambda f, i, be, bx, *_: (be[i], 0, nf + f)),
        ],
        out_specs=pl.BlockSpec((MOE_TM, MOE_TF), lambda f, i, *_: (i, f)),
        scratch_shapes=[
            pltpu.VMEM((2, 2, D_MODEL, MOE_TF), F32),
            pltpu.SemaphoreType.DMA((2, 2)),
        ],
    )
    return pl.pallas_call(
        _moe_up_kernel,
        grid_spec=grid_spec,
        out_shape=jax.ShapeDtypeStruct((n_slots, D_FF), BF16),
        compiler_params=_cparams(("arbitrary", "arbitrary")),
        name="moe_up",
    )(*meta, xs3, w1, b1, b1)


def _moe_down_kernel(be_ref, bx_ref, nr_ref, first_ref, seg_ref, nxt_ref, nseg_ref, a_ref, w_hbm, b_ref,
                     o_ref, wbuf_ref, sem):
    i = pl.program_id(0)
    nrows = nr_ref[i]

    def copies(e, slot):
        return [pltpu.make_async_copy(w_hbm.at[e], wbuf_ref.at[slot], sem.at[slot])]

    slot = _stream_expert_weights((be_ref, bx_ref, nr_ref, first_ref, seg_ref, nxt_ref, nseg_ref), i, copies)

    for h in range(MOE_TM // MOE_SUB):
        rows = slice(h * MOE_SUB, (h + 1) * MOE_SUB)

        @pl.when(nrows > h * MOE_SUB)
        def _(rows=rows):
            y = _dot_w(a_ref[rows, :], wbuf_ref[slot]) + b_ref[0]
            o_ref[rows] = y.astype(o_ref.dtype).reshape(MOE_SUB, ROW_SUB, LANES)

        @pl.when(nrows <= h * MOE_SUB)
        def _(rows=rows):
            o_ref[rows] = jnp.zeros((MOE_SUB, ROW_SUB, LANES), o_ref.dtype)


def _moe_down(meta, act, w2, b2, n_blocks):
    n_slots = act.shape[0]
    grid_spec = pltpu.PrefetchScalarGridSpec(
        num_scalar_prefetch=len(meta),
        grid=(n_blocks,),
        in_specs=[
            pl.BlockSpec((MOE_TM, D_FF), lambda i, be, bx, *_: (bx[i], 0)),
            pl.BlockSpec(memory_space=pl.ANY),
            pl.BlockSpec((1, 1, D_MODEL), lambda i, be, bx, *_: (be[i], 0, 0)),
        ],
        out_specs=pl.BlockSpec((MOE_TM, ROW_SUB, LANES), lambda i, *_: (i, 0, 0)),
        scratch_shapes=[
            pltpu.VMEM((2, D_FF, D_MODEL), F32),
            pltpu.SemaphoreType.DMA((2,)),
        ],
    )
    return pl.pallas_call(
        _moe_down_kernel,
        grid_spec=grid_spec,
        out_shape=jax.ShapeDtypeStruct((n_slots, ROW_SUB, LANES), BF16),
        compiler_params=_cparams(("arbitrary",)),
        name="moe_down",
    )(*meta, act, w2, b2)


def _combine_kernel(tab_ref, tabn_ref, pos_ref, gate_ref, y_ref, h_ref, o_ref, buf_ref, stage_ref, sem, *, tc):
    i = pl.program_id(0)
    slot = lax.rem(i, 2)

    def chunks(t_ref, s, start):
        def per_chunk(c, carry):
            cp = pltpu.make_async_copy(y_ref.at[pl.ds(t_ref[0, 0, 1 + c], COMBINE_CHUNK)],
                                       buf_ref.at[s, pl.ds(c * COMBINE_CHUNK, COMBINE_CHUNK)], sem.at[s])
            cp.start() if start else cp.wait()
            return carry

        lax.fori_loop(0, t_ref[0, 0, 0], per_chunk, 0)

    @pl.when(i == 0)
    def _():
        chunks(tab_ref, slot, True)

    @pl.when(i + 1 < pl.num_programs(0))
    def _():
        chunks(tabn_ref, 1 - slot, True)

    chunks(tab_ref, slot, False)

    def token(t, carry):
        acc = jnp.zeros((ROW_SUB, LANES), F32)
        for kk in range(TOP_K):
            j = t * TOP_K + kk
            acc = acc + gate_ref[0, 0, j] * buf_ref[slot, pos_ref[0, 0, j]].astype(F32)
        stage_ref[t] = acc
        return carry

    lax.fori_loop(0, tc, token, 0)
    o_ref[...] = h_ref[...] + stage_ref[...].reshape(tc, D_MODEL)


def _combine(tables, pos3, gate3, y3, h2, tc):
    t, d = h2.shape
    nt = t // tc
    n_tab = tables.shape[2]
    buf_rows = (n_tab - 1) * COMBINE_CHUNK
    smem = lambda shape, nxt: pl.BlockSpec(
        shape, (lambda i: (jnp.minimum(i + 1, nt - 1), 0, 0)) if nxt else (lambda i: (i, 0, 0)),
        memory_space=pltpu.SMEM)
    return pl.pallas_call(
        functools.partial(_combine_kernel, tc=tc),
        grid=(nt,),
        in_specs=[
            smem((1, 1, n_tab), False), smem((1, 1, n_tab), True),
            smem((1, 1, tc * TOP_K), False), smem((1, 1, tc * TOP_K), False),
            pl.BlockSpec(memory_space=pl.ANY),
            pl.BlockSpec((tc, d), lambda i: (i, 0)),
        ],
        out_specs=pl.BlockSpec((tc, d), lambda i: (i, 0)),
        out_shape=jax.ShapeDtypeStruct((t, d), F32),
        scratch_shapes=[
            pltpu.VMEM((2, buf_rows, ROW_SUB, LANES), BF16),
            pltpu.VMEM((tc, ROW_SUB, LANES), F32),
            pltpu.SemaphoreType.DMA((2,)),
        ],
        compiler_params=_cparams(("arbitrary",)),
        name="combine",
    )(tables, tables, pos3, gate3, y3, h2)


def _combine_plan(route, gates, slot_start, tc):
    t = route.shape[0]
    nt = t // tc
    idx = route[:, :TOP_K].reshape(nt, 1, tc * TOP_K)
    rank = route[:, TOP_K:2 * TOP_K].reshape(nt, tc * TOP_K)
    hit = idx == jnp.arange(N_EXPERTS, dtype=I32)[None, :, None]
    in_tile = jnp.sum(hit.astype(I32), axis=2)
    before = jnp.cumsum(in_tile, axis=0) - in_tile
    n_chunks = (in_tile + COMBINE_CHUNK - 1) // COMBINE_CHUNK
    c_end = jnp.cumsum(n_chunks, axis=1)
    c_start = c_end - n_chunks
    off = COMBINE_CHUNK * c_start
    pos = jnp.sum(jnp.where(hit, (off - before)[:, :, None], 0), axis=1) + rank
    max_chunks = (tc * TOP_K + N_EXPERTS * (COMBINE_CHUNK - 1)) // COMBINE_CHUNK
    c_ids = jnp.arange(max_chunks, dtype=I32)
    owner = jnp.logical_and(c_start[:, :, None] <= c_ids, c_ids < c_end[:, :, None])
    first = slot_start[None, :] + before - off
    src = jnp.sum(jnp.where(owner, first[:, :, None], 0), axis=1) + COMBINE_CHUNK * c_ids
    tables = jnp.concatenate([c_end[:, -1:], src], axis=1).astype(I32)
    gate3 = gates[:, :TOP_K].reshape(nt, 1, tc * TOP_K)
    return tables.reshape(nt, 1, 1 + max_chunks), pos.astype(I32).reshape(nt, 1, -1), gate3


def _moe_plan(counts, route, n_blocks):
    idx = route[:, :TOP_K].reshape(-1)
    rank = route[:, TOP_K:2 * TOP_K].reshape(-1)
    nblk = (counts + MOE_TM - 1) // MOE_TM
    blk_end = jnp.cumsum(nblk)
    blk_start = blk_end - nblk
    dest = (blk_start * MOE_TM)[idx] + rank
    total = blk_end[-1]
    ids = jnp.arange(n_blocks, dtype=I32)
    live = ids < total
    src = jnp.minimum(ids, total - 1)
    be = jnp.minimum(jnp.sum(blk_end[None, :] <= src[:, None], axis=1), N_EXPERTS - 1).astype(I32)
    nr = jnp.where(live, jnp.clip(counts[be] - (ids - blk_start[be]) * MOE_TM, 0, MOE_TM), 0).astype(I32)
    first = jnp.logical_and(live, ids == blk_start[be]).astype(I32)
    has = counts > 0
    ordinal = jnp.cumsum(has.astype(I32)) - 1
    e_ids = jnp.arange(N_EXPERTS, dtype=I32)
    later = jnp.logical_and(has[None, :], e_ids[None, :] > e_ids[:, None])
    nxt_e = jnp.min(jnp.where(later, e_ids[None, :], N_EXPERTS), axis=1)
    nxt_e = jnp.where(nxt_e == N_EXPERTS, jnp.argmax(has).astype(I32), nxt_e)
    nseg = jnp.sum(has.astype(I32)).reshape(1)
    meta = (be, src.astype(I32), nr, first, ordinal[be].astype(I32), nxt_e[be].astype(I32), nseg)
    return dest.astype(I32), meta, (blk_start * MOE_TM).astype(I32)


def _layer(h, mem, lam_init, norm_mix_g, w_in, conv_w, conv_b, dt_bias, a_log, d_skip, ssd_norm_g,
           da_q_norm_g, da_k_norm_g, lq1, lk1, lq2, lk2, da_subln_g, w_out, norm_xa_g, norm_mem_g,
           xa_wq, xa_wkv, xa_q_norm_g, xa_k_norm_g, xa_wo, norm_ffn_g, router_w, router_b,
           moe_w1, moe_b1, moe_w2, moe_b2):
    bsz, seq, d = h.shape
    t = bsz * seq
    mem_len = mem.shape[1]
    row = lambda v: v.astype(F32).reshape(1, -1)
    x2 = h.reshape(t, d)

    o1 = SSD_WIDTH + XBC_WIDTH
    o2 = o1 + SSD_HEADS
    w_main = _w_prep(w_in)
    w_dt = _split3_host(jnp.pad(w_in[:, o1:o2], ((0, 0), (0, LANES - SSD_HEADS))))
    proj, dt_raw = _in_proj(x2, row(norm_mix_g), w_main, w_dt)
    proj3 = proj.reshape(bsz, seq, PROJ_COLS)
    y_ssd = _ssd(proj3, dt_raw.reshape(bsz, seq, LANES), conv_w, conv_b, dt_bias, a_log, d_skip, ssd_norm_g)

    gq = row(jnp.tile(da_q_norm_g.astype(F32), 2) * (DA_HALF ** -0.5))
    gk = row(jnp.tile(da_k_norm_g.astype(F32), 2))
    lvec = jnp.stack([lq1, lk1, lq2, lk2]).astype(F32)
    y_da = _diff_attn(proj3, gq, gk, lvec, row(da_subln_g), lam_init)

    h1 = _mm_res([y_ssd.reshape(t, SSD_WIDTH), y_da.reshape(t, DA_WIDTH)], w_out.astype(BF16), x2)

    eq = jnp.tile(xa_q_norm_g.astype(F32) * (XA_HEAD_DIM ** -0.5), XA_HEADS).reshape(1, -1)
    qx = _norm_proj(h1, row(norm_xa_g), xa_wq.astype(BF16), eq[:, :XA_HEAD_DIM], XA_HEADS, 1024, XA_HEAD_DIM)
    kv = _norm_proj(mem.reshape(bsz * mem_len, d), row(norm_mem_g), xa_wkv.astype(BF16),
                    row(xa_k_norm_g), XA_HEADS, bsz * mem_len, XA_HEAD_DIM)
    ox = _xattn(qx, kv, seq, mem_len)
    h2 = _mm_res([ox], xa_wo.astype(BF16), h1)

    rw3 = _split3_host(jnp.pad(router_w, ((0, 0), (0, LANES - N_EXPERTS))))
    rb = jnp.pad(router_b.astype(F32), (0, LANES - N_EXPERTS), constant_values=-jnp.inf).reshape(1, LANES)
    hf, route, gates, cnt = _router(h2, row(norm_ffn_g), rw3, rb)
    n_blocks = (t * TOP_K) // MOE_TM + N_EXPERTS + 1
    n_slots = n_blocks * MOE_TM
    dest, meta, slot_start = _moe_plan(cnt[0, :N_EXPERTS].astype(I32), route, n_blocks)
    tt = 512
    xs = _dispatch(dest.reshape(t // tt, 1, tt * TOP_K), meta[2], hf, n_blocks, tt)
    act = _moe_up(meta, xs, moe_w1,
                  moe_b1.reshape(N_EXPERTS, 1, 2 * D_FF), n_blocks)
    y = _moe_down(meta, act, moe_w2, moe_b2.reshape(N_EXPERTS, 1, D_MODEL), n_blocks)
    tc = 256
    out = _combine(*_combine_plan(route, gates, slot_start, tc), y, h2, tc)
    return out.reshape(bsz, seq, d)


def kernel(x, mem, norm_mix_g, w_in, conv_w, conv_b, dt_bias, a_log, d_skip, ssd_norm_g, da_q_norm_g,
           da_k_norm_g, lambda_q1, lambda_k1, lambda_q2, lambda_k2, da_subln_g, w_out, norm_xa_g,
           norm_mem_g, xa_wq, xa_wkv, xa_q_norm_g, xa_k_norm_g, xa_wo, norm_ffn_g, router_w, router_b,
           moe_w1, moe_b1, moe_w2, moe_b2):
    h = x
    for layer in range(norm_mix_g.shape[0]):
        lam_init = 0.8 - 0.6 * math.exp(-0.3 * layer)
        h = _layer(h, mem, lam_init, norm_mix_g[layer], w_in[layer], conv_w[layer], conv_b[layer],
                   dt_bias[layer], a_log[layer], d_skip[layer], ssd_norm_g[layer], da_q_norm_g[layer],
                   da_k_norm_g[layer], lambda_q1[layer], lambda_k1[layer], lambda_q2[layer],
                   lambda_k2[layer], da_subln_g[layer], w_out[layer], norm_xa_g[layer], norm_mem_g[layer],
                   xa_wq[layer], xa_wkv[layer], xa_q_norm_g[layer], xa_k_norm_g[layer], xa_wo[layer],
                   norm_ffn_g[layer], router_w[layer], router_b[layer], moe_w1[layer], moe_b1[layer],
                   moe_w2[layer], moe_b2[layer])
    return h
```

```python
import functools
import math

import jax
import jax.numpy as jnp
from jax import lax
from jax.experimental import pallas as pl
from jax.experimental.pallas import tpu as pltpu

F32 = jnp.float32
BF16 = jnp.bfloat16
I32 = jnp.int32

D_MODEL = 2048
SSD_WIDTH = 1024
SSD_HEAD_DIM = 64
SSD_HEADS = 16
SSD_GROUPS = 2
SSD_STATE = 128
SSD_CONV = 4
SSD_CHUNK = 128
XBC_WIDTH = SSD_WIDTH + 2 * SSD_GROUPS * SSD_STATE
DA_WIDTH = 1024
DA_HEAD_DIM = 128
DA_HALF = 64
DA_HEADS = 8
XA_HEADS = 4
XA_HEAD_DIM = D_MODEL // XA_HEADS
N_EXPERTS = 32
TOP_K = 4
D_FF = 2048
SWIGLU_LIMIT = 7.0
SWIGLU_ALPHA = 1.702
EPS = 1e-6

LANES = 128
ROW_SUB = D_MODEL // LANES
VMEM_LIMIT = 56 * 1024 * 1024

PROJ_COLS = SSD_WIDTH + XBC_WIDTH + 3 * DA_WIDTH
COL_Q = (SSD_WIDTH + XBC_WIDTH) // LANES
COL_V = COL_Q + 2 * DA_HEADS

MOE_TM = 512
MOE_TF = 1024
COMBINE_CHUNK = 8


def _dot(a, b):
    return jnp.dot(a, b, preferred_element_type=F32)


def _dot_nt(a, b):
    return lax.dot_general(a, b, (((1,), (1,)), ((), ())), preferred_element_type=F32)


def _dot_tn(a, b):
    return lax.dot_general(a, b, (((0,), (0,)), ((), ())), preferred_element_type=F32)


def _rms(x):
    return x * lax.rsqrt(jnp.mean(x * x, axis=-1, keepdims=True) + EPS)


def _split2(x):
    hi = x.astype(BF16)
    mid = (x - hi.astype(F32)).astype(BF16)
    return hi, mid


def _split3(x):
    hi = x.astype(BF16)
    r = x - hi.astype(F32)
    mid = r.astype(BF16)
    lo = (r - mid.astype(F32)).astype(BF16)
    return hi, mid, lo


def _split3_host(w):
    hi, mid, lo = _split3(w.astype(F32))
    return jnp.stack([hi, mid, lo])


def _silu(x):
    return x * jax.nn.sigmoid(x)


def _cparams(sem):
    return pltpu.CompilerParams(dimension_semantics=sem, vmem_limit_bytes=VMEM_LIMIT)


def _w_prep_kernel(w_ref, o_ref):
    x = w_ref[...]
    o1 = SSD_WIDTH + XBC_WIDTH
    o_ref[...] = jnp.concatenate([x[:, :o1], x[:, o1 + SSD_HEADS:]], axis=1).astype(o_ref.dtype)


def _w_prep(w_in, tr=256):
    k, n = w_in.shape
    return pl.pallas_call(
        _w_prep_kernel,
        grid=(k // tr,),
        in_specs=[pl.BlockSpec((tr, n), lambda i: (i, 0))],
        out_specs=pl.BlockSpec((tr, PROJ_COLS), lambda i: (i, 0)),
        out_shape=jax.ShapeDtypeStruct((k, PROJ_COLS), BF16),
        compiler_params=_cparams(("parallel",)),
        name="w_prep",
    )(w_in)


def _in_proj_kernel(x_ref, g_ref, w_ref, wdt_ref, o_ref, dt_ref, xn_ref):
    @pl.when(pl.program_id(1) == 0)
    def _():
        hn = _rms(x_ref[...]) * g_ref[...]
        hi, mid = _split2(hn)
        xn_ref[...] = hi
        dt_ref[...] = _dot(hi, wdt_ref[0]) + (_dot(hi, wdt_ref[1]) + _dot(mid, wdt_ref[0]))

    o_ref[...] = _dot(xn_ref[...], w_ref[...]).astype(o_ref.dtype)


def _in_proj(x2, g, w, wdt3, tm=1024, tn=512):
    m, k = x2.shape
    n = w.shape[1]
    return pl.pallas_call(
        _in_proj_kernel,
        grid=(m // tm, n // tn),
        in_specs=[
            pl.BlockSpec((tm, k), lambda i, j: (i, 0)),
            pl.BlockSpec((1, k), lambda i, j: (0, 0)),
            pl.BlockSpec((k, tn), lambda i, j: (0, j)),
            pl.BlockSpec((3, k, LANES), lambda i, j: (0, 0, 0)),
        ],
        out_specs=[
            pl.BlockSpec((tm, tn), lambda i, j: (i, j)),
            pl.BlockSpec((tm, LANES), lambda i, j: (i, 0)),
        ],
        out_shape=[jax.ShapeDtypeStruct((m, n), BF16), jax.ShapeDtypeStruct((m, LANES), F32)],
        scratch_shapes=[pltpu.VMEM((tm, k), BF16)],
        compiler_params=_cparams(("parallel", "arbitrary")),
        name="in_proj",
    )(x2, g, w, wdt3)


def _norm_proj_kernel(x_ref, g_ref, w_ref, eg_ref, o_ref, xn_ref, *, epi_tiles):
    j = pl.program_id(1)

    @pl.when(j == 0)
    def _():
        xn_ref[...] = (_rms(x_ref[...]) * g_ref[...]).astype(BF16)

    acc = _dot(xn_ref[...], w_ref[...])

    @pl.when(j < epi_tiles)
    def _():
        o_ref[...] = (_rms(acc) * eg_ref[...]).astype(o_ref.dtype)

    @pl.when(j >= epi_tiles)
    def _():
        o_ref[...] = acc.astype(o_ref.dtype)


def _norm_proj(x2, g, w, eg, epi_tiles, tm, tn):
    m, k = x2.shape
    n = w.shape[1]
    return pl.pallas_call(
        functools.partial(_norm_proj_kernel, epi_tiles=epi_tiles),
        grid=(m // tm, n // tn),
        in_specs=[
            pl.BlockSpec((tm, k), lambda i, j: (i, 0)),
            pl.BlockSpec((1, k), lambda i, j: (0, 0)),
            pl.BlockSpec((k, tn), lambda i, j: (0, j)),
            pl.BlockSpec((1, tn), lambda i, j: (0, 0)),
        ],
        out_specs=pl.BlockSpec((tm, tn), lambda i, j: (i, j)),
        out_shape=jax.ShapeDtypeStruct((m, n), BF16),
        scratch_shapes=[pltpu.VMEM((tm, k), BF16)],
        compiler_params=_cparams(("parallel", "arbitrary")),
        name="norm_proj",
    )(x2, g, w, eg)


def _mm_res_kernel(*refs, n_a):
    a_refs, w_refs = refs[:n_a], refs[n_a:2 * n_a]
    res_ref, o_ref = refs[2 * n_a], refs[2 * n_a + 1]
    acc = res_ref[...]
    for a, w in zip(a_refs, w_refs):
        acc = acc + _dot(a[...], w[...])
    o_ref[...] = acc


def _mm_res(a_list, w, res, tm=1024, tn=1024):
    m, n = res.shape
    n_a = len(a_list)
    in_specs = []
    for a in a_list:
        in_specs.append(pl.BlockSpec((tm, a.shape[1]), lambda i, j: (i, 0)))
    for idx, a in enumerate(a_list):
        in_specs.append(pl.BlockSpec((a.shape[1], tn), lambda i, j, idx=idx: (idx, j)))
    in_specs.append(pl.BlockSpec((tm, tn), lambda i, j: (i, j)))
    return pl.pallas_call(
        functools.partial(_mm_res_kernel, n_a=n_a),
        grid=(m // tm, n // tn),
        in_specs=in_specs,
        out_specs=pl.BlockSpec((tm, tn), lambda i, j: (i, j)),
        out_shape=jax.ShapeDtypeStruct((m, n), F32),
        compiler_params=_cparams(("parallel", "arbitrary")),
        name="mm_res",
    )(*a_list, *([w] * n_a), res)


def _ssd_kernel(z_ref, xs_ref, b_ref, c_ref, dtr_ref, cw_ref, cb_ref, dtb_ref, alog_ref,
                dsk_ref, ng_ref, sel_ref, o_ref, ext_ref, ht_ref, y_ref, *, nb):
    L = SSD_CHUNK

    @pl.when(pl.program_id(1) == 0)
    def _():
        ht_ref[...] = jnp.zeros_like(ht_ref)
        ext_ref[:, 0:8, :] = jnp.zeros((nb, 8, XBC_WIDTH), F32)

    row = lax.broadcasted_iota(I32, (L, L), 0)
    col = lax.broadcasted_iota(I32, (L, L), 1)
    causal = col <= row
    tril = jnp.where(causal, 1.0, 0.0).astype(BF16)
    lane = lax.broadcasted_iota(I32, (L, LANES), 1)
    first = lane < SSD_HEAD_DIM
    a = -jnp.exp(alog_ref[...])

    for bb in range(nb):
        _ssd_chunk(bb, z_ref, xs_ref, b_ref, c_ref, dtr_ref, cw_ref, cb_ref, dtb_ref, a, dsk_ref, ng_ref,
                   sel_ref, o_ref, ext_ref, ht_ref, y_ref, causal, tril, first)


def _ssd_chunk(bb, z_ref, xs_ref, b_ref, c_ref, dtr_ref, cw_ref, cb_ref, dtb_ref, a, dsk_ref, ng_ref,
               sel_ref, o_ref, ext_ref, ht_ref, y_ref, causal, tril, first):
    L = SSD_CHUNK
    n_pairs = SSD_HEADS // 2
    pairs_per_group = n_pairs // SSD_GROUPS
    c_off = SSD_WIDTH + SSD_GROUPS * SSD_STATE
    ext = ext_ref.at[bb]
    ext[8:8 + L, 0:SSD_WIDTH] = xs_ref[bb].astype(F32)
    ext[8:8 + L, SSD_WIDTH:c_off] = b_ref[bb].astype(F32)
    ext[8:8 + L, c_off:XBC_WIDTH] = c_ref[bb].astype(F32)
    acc = jnp.broadcast_to(cb_ref[...], (L, XBC_WIDTH))
    for k in range(SSD_CONV):
        acc = acc + cw_ref[k:k + 1, :] * ext[pl.ds(8 - (SSD_CONV - 1) + k, L), :]
    ext[0:8, :] = ext[L:L + 8, :]
    xbc = _silu(acc)

    dt = jax.nn.softplus(dtr_ref[bb] + dtb_ref[...])
    adt = dt * a
    a_hi, a_mid, a_lo = _split3(adt)
    acum = _dot(tril, a_hi) + (_dot(tril, a_mid) + _dot(tril, a_lo))
    acum_t = acum.T
    pieces = [jnp.concatenate([p, q], axis=0) for p, q in zip(_split3(acum), _split3(dt))]

    cbs = []
    for g in range(SSD_GROUPS):
        bg = xbc[:, SSD_WIDTH + g * SSD_STATE:SSD_WIDTH + (g + 1) * SSD_STATE].astype(BF16)
        cg = xbc[:, c_off + g * SSD_STATE:c_off + (g + 1) * SSD_STATE].astype(BF16)
        cbs.append((bg, cg, _dot_nt(cg, bg)))

    for p in range(n_pairs):
        bg, cg, cb = cbs[p // pairs_per_group]
        sel = sel_ref[:, 2 * LANES * p:2 * LANES * (p + 1)]
        fb = _dot(pieces[0], sel) + (_dot(pieces[1], sel) + _dot(pieces[2], sel))
        ms = []
        for i in range(2):
            h = 2 * p + i
            seg = fb[:L, i * LANES:(i + 1) * LANES] - acum_t[h:h + 1, :]
            dec = jnp.exp(jnp.where(causal, seg, -jnp.inf))
            ms.append((cb * dec).astype(BF16))
        ac = jnp.where(first, fb[:L, :LANES], fb[:L, LANES:])
        dtp = jnp.where(first, fb[L:, :LANES], fb[L:, LANES:])
        xp = xbc[:, p * LANES:(p + 1) * LANES]
        xdt = xp * dtp
        y = _dot(ms[0], jnp.where(first, xdt, 0.0).astype(BF16))
        y = y + _dot(ms[1], jnp.where(first, 0.0, xdt).astype(BF16))
        ht = ht_ref[bb * n_pairs + p]
        y = y + _dot(cg, ht.astype(BF16)) * jnp.exp(ac)
        a_last = ac[L - 1:L, :]
        ht_ref[bb * n_pairs + p] = (ht * jnp.exp(a_last)
                                    + _dot_tn(bg, (xdt * jnp.exp(a_last - ac)).astype(BF16)))
        y_ref[bb, :, p * LANES:(p + 1) * LANES] = y + xp * dsk_ref[:, p * LANES:(p + 1) * LANES]

    yg = y_ref[bb] * _silu(z_ref[bb].astype(F32))
    gw = SSD_WIDTH // SSD_GROUPS
    for g in range(SSD_GROUPS):
        blk = _rms(yg[:, g * gw:(g + 1) * gw]) * ng_ref[:, g * gw:(g + 1) * gw]
        o_ref[bb, :, g * gw:(g + 1) * gw] = blk.astype(o_ref.dtype)


def _ssd(proj3, dt3, conv_w, conv_b, dt_bias, a_log, d_skip, norm_g, nb=2):
    bsz, s, _ = proj3.shape
    L = SSD_CHUNK
    pad = LANES - SSD_HEADS
    dtb = jnp.pad(dt_bias.astype(F32), (0, pad)).reshape(1, LANES)
    alog = jnp.pad(a_log.astype(F32), (0, pad)).reshape(1, LANES)
    dsk = jnp.repeat(d_skip.astype(F32), SSD_HEAD_DIM).reshape(1, SSD_WIDTH)
    sel = (jnp.arange(LANES)[:, None] == (jnp.arange(SSD_HEADS * LANES)[None, :] // LANES)).astype(BF16)
    const = lambda shape: pl.BlockSpec(shape, lambda b, c: (0,) * len(shape))
    return pl.pallas_call(
        functools.partial(_ssd_kernel, nb=nb),
        grid=(bsz // nb, s // L),
        in_specs=[
            pl.BlockSpec((nb, L, SSD_WIDTH), lambda b, c: (b, c, 0)),
            pl.BlockSpec((nb, L, SSD_WIDTH), lambda b, c: (b, c, 1)),
            pl.BlockSpec((nb, L, 2 * SSD_STATE), lambda b, c: (b, c, 2 * SSD_WIDTH // (2 * SSD_STATE))),
            pl.BlockSpec((nb, L, 2 * SSD_STATE), lambda b, c: (b, c, 2 * SSD_WIDTH // (2 * SSD_STATE) + 1)),
            pl.BlockSpec((nb, L, LANES), lambda b, c: (b, c, 0)),
            const((SSD_CONV, XBC_WIDTH)), const((1, XBC_WIDTH)), const((1, LANES)), const((1, LANES)),
            const((1, SSD_WIDTH)), const((1, SSD_WIDTH)), const((LANES, SSD_HEADS * LANES)),
        ],
        out_specs=pl.BlockSpec((nb, L, SSD_WIDTH), lambda b, c: (b, c, 0)),
        out_shape=jax.ShapeDtypeStruct((bsz, s, SSD_WIDTH), BF16),
        scratch_shapes=[
            pltpu.VMEM((nb, L + 8, XBC_WIDTH), F32),
            pltpu.VMEM((nb * SSD_HEADS // 2, SSD_STATE, LANES), F32),
            pltpu.VMEM((nb, L, SSD_WIDTH), F32),
        ],
        compiler_params=_cparams(("parallel", "arbitrary")),
        name="ssd",
    )(proj3, proj3, proj3, proj3, dt3, conv_w.astype(F32), conv_b.astype(F32).reshape(1, XBC_WIDTH),
      dtb, alog, dsk, norm_g.astype(F32).reshape(1, SSD_WIDTH), sel)


def _half_norm(x, g):
    ra = lax.broadcasted_iota(I32, (LANES, LANES), 0) // DA_HALF
    rb = lax.broadcasted_iota(I32, (LANES, LANES), 1) // DA_HALF
    grp = jnp.where(ra == rb, 1.0, 0.0).astype(BF16)
    hi, mid = _split2(x * x)
    ss = _dot(hi, grp) + _dot(mid, grp)
    return x * lax.rsqrt(ss * (1.0 / DA_HALF) + EPS) * g


def _da_kernel(q_ref, k_ref, v_ref, gq_ref, gk_ref, lv_ref, sg_ref, o_ref,
               kn_ref, va_ref, q2_ref, m_ref, l_ref, acc_ref, *, lam_init, tq, seq):
    qi = pl.program_id(2)
    tk = tq

    @pl.when(qi == 0)
    def _():
        for c in range(seq // tk):
            rows = slice(c * tk, (c + 1) * tk)
            kn_ref[rows, :] = _half_norm(k_ref[0, rows, :].astype(F32), gk_ref[...]).astype(BF16)
            va_ref[rows, :LANES] = v_ref[0, rows, :]
            va_ref[rows, LANES:] = jnp.ones((tk, LANES), BF16)

    q = _half_norm(q_ref[0].astype(F32), gq_ref[...]).astype(BF16)
    first = lax.broadcasted_iota(I32, q.shape, 1) < DA_HALF
    zero = jnp.zeros_like(q)
    q2_ref[0] = jnp.where(first, q, zero)
    q2_ref[1] = jnp.where(first, zero, q)
    m_ref[...] = jnp.full(m_ref.shape, -jnp.inf, F32)
    l_ref[...] = jnp.zeros_like(l_ref)
    acc_ref[...] = jnp.zeros_like(acc_ref)

    def step(c, masked):
        k = kn_ref[c * tk:(c + 1) * tk, :]
        va = va_ref[c * tk:(c + 1) * tk, :]
        if masked:
            keep = (lax.broadcasted_iota(I32, (tq, tk), 1) <= lax.broadcasted_iota(I32, (tq, tk), 0))
        for j in range(2):
            s = _dot_nt(q2_ref[j], k)
            if masked:
                s = jnp.where(keep, s, -jnp.inf)
            m_prev = m_ref[j]
            m_new = jnp.maximum(m_prev, jnp.max(s, axis=-1, keepdims=True))
            alpha = jnp.exp(m_prev - m_new)
            p = jnp.exp(s - jnp.concatenate([m_new] * (tk // LANES), axis=1))
            pv = _dot(p.astype(BF16), va)
            l_ref[j] = alpha * l_ref[j] + pv[:, LANES:]
            acc_ref[j] = alpha * acc_ref[j] + pv[:, :LANES]
            m_ref[j] = m_new

    for qv in range(seq // tq):
        @pl.when(qi == qv)
        def _(qv=qv):
            for c in range(qv):
                step(c, False)
            step(qv, True)

    lv = lv_ref[...]
    lam = (jnp.exp(jnp.sum(lv[0:1] * lv[1:2], axis=-1, keepdims=True))
           - jnp.exp(jnp.sum(lv[2:3] * lv[3:4], axis=-1, keepdims=True)) + lam_init)
    o = acc_ref[0] / l_ref[0] - lam * (acc_ref[1] / l_ref[1])
    o_ref[0] = (_rms(o) * (sg_ref[...] * (1.0 - lam_init))).astype(o_ref.dtype)


def _diff_attn(proj3, gq, gk, lvec, subln_g, lam_init, tq=512):
    bsz, s, _ = proj3.shape
    const = lambda shape: pl.BlockSpec(shape, lambda b, h, qi: (0,) * len(shape))
    return pl.pallas_call(
        functools.partial(_da_kernel, lam_init=lam_init, tq=tq, seq=s),
        grid=(bsz, DA_HEADS, s // tq),
        in_specs=[
            pl.BlockSpec((1, tq, LANES), lambda b, h, qi: (b, qi, COL_Q + h)),
            pl.BlockSpec((1, s, LANES), lambda b, h, qi: (b, 0, COL_Q + DA_HEADS + h)),
            pl.BlockSpec((1, s, LANES), lambda b, h, qi: (b, 0, COL_V + h)),
            const((1, LANES)), const((1, LANES)), const((4, DA_HALF)), const((1, LANES)),
        ],
        out_specs=pl.BlockSpec((1, tq, LANES), lambda b, h, qi: (b, qi, h)),
        out_shape=jax.ShapeDtypeStruct((bsz, s, DA_WIDTH), BF16),
        scratch_shapes=[
            pltpu.VMEM((s, LANES), BF16),
            pltpu.VMEM((s, 2 * LANES), BF16),
            pltpu.VMEM((2, tq, LANES), BF16),
            pltpu.VMEM((2, tq, LANES), F32),
            pltpu.VMEM((2, tq, LANES), F32),
            pltpu.VMEM((2, tq, LANES), F32),
        ],
        compiler_params=_cparams(("parallel", "parallel", "arbitrary")),
        name="diff_attn",
    )(proj3, proj3, proj3, gq, gk, lvec, subln_g)


def _xattn_kernel(q_ref, k_ref, v_ref, o_ref):
    s = _dot_nt(q_ref[...], k_ref[...])
    p = jnp.exp(s - jnp.max(s, axis=-1, keepdims=True))
    o = _dot(p.astype(BF16), v_ref[...]) / jnp.sum(p, axis=-1, keepdims=True)
    o_ref[...] = o.astype(o_ref.dtype)


def _xattn(qx, kv, seq, mem_len, tm=512):
    m = qx.shape[0]
    hd = XA_HEAD_DIM
    return pl.pallas_call(
        _xattn_kernel,
        grid=(m // tm, XA_HEADS),
        in_specs=[
            pl.BlockSpec((tm, hd), lambda i, j: (i, j)),
            pl.BlockSpec((mem_len, hd), lambda i, j: ((i * tm) // seq, j)),
            pl.BlockSpec((mem_len, hd), lambda i, j: ((i * tm) // seq, XA_HEADS + j)),
        ],
        out_specs=pl.BlockSpec((tm, hd), lambda i, j: (i, j)),
        out_shape=jax.ShapeDtypeStruct((m, XA_HEADS * hd), BF16),
        compiler_params=_cparams(("parallel", "parallel")),
        name="xattn",
    )(qx, kv, kv)


def _router_kernel(x_ref, g_ref, rw_ref, rb_ref, hf_ref, route_ref, gate_ref, cnt_ref, carry_ref, *, tm):
    i = pl.program_id(0)

    @pl.when(i == 0)
    def _():
        carry_ref[...] = jnp.zeros_like(carry_ref)

    hn = _rms(x_ref[...]) * g_ref[...]
    hi, mid = _split2(hn)
    hf_ref[...] = hi.reshape(tm, ROW_SUB, LANES)

    logits = _dot(hi, rw_ref[0]) + (_dot(hi, rw_ref[1]) + _dot(mid, rw_ref[0])) + rb_ref[...]
    lane = lax.broadcasted_iota(I32, (tm, LANES), 1)
    lane_f = lane.astype(F32)
    idxs, vals = [], []
    work = logits
    for _ in range(TOP_K):
        m = jnp.max(work, axis=-1, keepdims=True)
        ik = jnp.min(jnp.where(work == m, lane_f, float(LANES)), axis=-1, keepdims=True)
        idxs.append(ik)
        vals.append(m)
        work = jnp.where(lane_f == ik, -jnp.inf, work)
    es = [jnp.exp(v - vals[0]) for v in vals]
    den = es[0] + es[1] + es[2] + es[3]

    onehot = jnp.zeros((tm, LANES), F32)
    for ik in idxs:
        onehot = onehot + jnp.where(lane_f == ik, 1.0, 0.0)
    r = lax.broadcasted_iota(I32, (tm, tm), 0)
    c = lax.broadcasted_iota(I32, (tm, tm), 1)
    strict = jnp.where(c < r, 1.0, 0.0).astype(BF16)
    before = carry_ref[0:1, :] + _dot(strict, onehot.astype(BF16))
    carry_ref[0:1, :] = carry_ref[0:1, :] + jnp.sum(onehot, axis=0, keepdims=True)
    cnt_ref[...] = jnp.broadcast_to(carry_ref[0:1, :], cnt_ref.shape)

    route = jnp.zeros((tm, LANES), F32)
    gates = jnp.zeros((tm, LANES), F32)
    for kk in range(TOP_K):
        rank = jnp.sum(jnp.where(lane_f == idxs[kk], before, 0.0), axis=-1, keepdims=True)
        route = jnp.where(lane == kk, idxs[kk], route)
        route = jnp.where(lane == TOP_K + kk, rank, route)
        gates = jnp.where(lane == kk, es[kk] / den, gates)
    route_ref[...] = route.astype(I32)
    gate_ref[...] = gates


def _router(h2, g, rw3, rb, tm=512):
    t, d = h2.shape
    return pl.pallas_call(
        functools.partial(_router_kernel, tm=tm),
        grid=(t // tm,),
        in_specs=[
            pl.BlockSpec((tm, d), lambda i: (i, 0)),
            pl.BlockSpec((1, d), lambda i: (0, 0)),
            pl.BlockSpec((3, d, LANES), lambda i: (0, 0, 0)),
            pl.BlockSpec((1, LANES), lambda i: (0, 0)),
        ],
        out_specs=[
            pl.BlockSpec((tm, ROW_SUB, LANES), lambda i: (i, 0, 0)),
            pl.BlockSpec((tm, LANES), lambda i: (i, 0)),
            pl.BlockSpec((tm, LANES), lambda i: (i, 0)),
            pl.BlockSpec((8, LANES), lambda i: (0, 0)),
        ],
        out_shape=[
            jax.ShapeDtypeStruct((t, ROW_SUB, LANES), BF16),
            jax.ShapeDtypeStruct((t, LANES), I32),
            jax.ShapeDtypeStruct((t, LANES), F32),
            jax.ShapeDtypeStruct((8, LANES), F32),
        ],
        scratch_shapes=[pltpu.VMEM((8, LANES), F32)],
        compiler_params=_cparams(("arbitrary",)),
        name="router",
    )(h2, g, rw3, rb)


def _dispatch_kernel(dest_ref, nr_ref, hf_ref, xs_ref, zero_ref, sem, zsem, *, tt, n_blocks):
    @pl.when(pl.program_id(0) == 0)
    def _():
        zero_ref[...] = jnp.zeros_like(zero_ref)

        def fill(start):
            def body(b, carry):
                pad = MOE_TM - nr_ref[b]
                off = b * MOE_TM + nr_ref[b]
                size = MOE_TM
                while size >= 1:
                    hit = (pad & size) != 0
                    cp = pltpu.make_async_copy(zero_ref.at[pl.ds(0, size)], xs_ref.at[pl.ds(off, size)], zsem)

                    @pl.when(hit)
                    def _(cp=cp):
                        cp.start() if start else cp.wait()

                    off = off + jnp.where(hit, size, 0)
                    size //= 2
                return carry
            lax.fori_loop(0, n_blocks, body, 0)

        fill(True)
        fill(False)

    def copy(t, kk):
        return pltpu.make_async_copy(hf_ref.at[t], xs_ref.at[dest_ref[0, 0, t * TOP_K + kk]], sem)

    def issue(t, carry):
        for kk in range(TOP_K):
            copy(t, kk).start(priority=kk % 2)
        return carry

    def drain(t, carry):
        for kk in range(TOP_K):
            copy(t, kk).wait()
        return carry

    lax.fori_loop(0, tt, issue, 0)
    lax.fori_loop(0, tt, drain, 0)


def _dispatch(dest2, nr, hf3, n_blocks, tt):
    t = hf3.shape[0]
    return pl.pallas_call(
        functools.partial(_dispatch_kernel, tt=tt, n_blocks=n_blocks),
        grid=(t // tt,),
        in_specs=[
            pl.BlockSpec((1, 1, tt * TOP_K), lambda i: (i, 0, 0), memory_space=pltpu.SMEM),
            pl.BlockSpec(memory_space=pltpu.SMEM),
            pl.BlockSpec((tt, ROW_SUB, LANES), lambda i: (i, 0, 0)),
        ],
        out_specs=pl.BlockSpec(memory_space=pl.ANY),
        out_shape=jax.ShapeDtypeStruct((n_blocks * MOE_TM, ROW_SUB, LANES), BF16),
        scratch_shapes=[pltpu.VMEM((MOE_TM, ROW_SUB, LANES), BF16), pltpu.SemaphoreType.DMA(()),
                        pltpu.SemaphoreType.DMA(())],
        compiler_params=_cparams(("arbitrary",)),
        name="dispatch",
    )(dest2, nr, hf3)


def _stream_expert_weights(meta_refs, i, copies):
    be_ref, _, _, first_ref, seg_ref, nxt_ref, nseg_ref = meta_refs
    k = seg_ref[i]
    slot = lax.rem(k, 2)

    @pl.when(first_ref[i] == 1)
    def _():
        @pl.when(k == 0)
        def _():
            for c in copies(be_ref[i], slot):
                c.start()

        @pl.when(k < nseg_ref[0] - 1)
        def _():
            for c in copies(nxt_ref[i], 1 - slot):
                c.start()

        for c in copies(be_ref[i], slot):
            c.wait()

    return slot


def _dot_w(x, w):
    return lax.dot_general(x, w, (((1,), (0,)), ((), ())), preferred_element_type=F32)


def _block_cases(nrows):
    q = MOE_TM // 4
    return [
        (nrows > 3 * q, [(0, MOE_TM)], MOE_TM),
        (jnp.logical_and(nrows > 2 * q, nrows <= 3 * q), [(0, 2 * q), (2 * q, q)], 3 * q),
        (jnp.logical_and(nrows > q, nrows <= 2 * q), [(0, 2 * q)], 2 * q),
        (jnp.logical_and(nrows > 0, nrows <= q), [(0, q)], q),
        (nrows <= 0, [], 0),
    ]


def _moe_up_kernel(be_ref, bx_ref, nr_ref, first_ref, seg_ref, nxt_ref, nseg_ref, x_ref, w_hbm, bg_ref,
                   bu_ref, o_ref, wbuf_ref, sem):
    f = pl.program_id(0)
    i = pl.program_id(1)
    nrows = nr_ref[i]
    nf = D_FF // MOE_TF

    def copies(e, slot):
        out = []
        for part in range(2):
            col = pl.multiple_of((part * nf + f) * MOE_TF, MOE_TF)
            out.append(pltpu.make_async_copy(w_hbm.at[e, :, pl.ds(col, MOE_TF)], wbuf_ref.at[slot, part],
                                             sem.at[slot, part]))
        return out

    slot = _stream_expert_weights((be_ref, bx_ref, nr_ref, first_ref, seg_ref, nxt_ref, nseg_ref), i, copies)

    for cond, pieces, dead in _block_cases(nrows):
        @pl.when(cond)
        def _(pieces=pieces, dead=dead):
            for start, size in pieces:
                rows = slice(start, start + size)
                xb = x_ref[rows].reshape(size, D_MODEL)
                g = jnp.minimum(_dot_w(xb, wbuf_ref[slot, 0]) + bg_ref[0], SWIGLU_LIMIT)
                u = jnp.clip(_dot_w(xb, wbuf_ref[slot, 1]) + bu_ref[0], -SWIGLU_LIMIT, SWIGLU_LIMIT)
                act = g * jax.nn.sigmoid(SWIGLU_ALPHA * g) * (u + 1.0)
                o_ref[rows, :] = act.astype(o_ref.dtype)
            if dead < MOE_TM:
                o_ref[dead:, :] = jnp.zeros((MOE_TM - dead, MOE_TF), o_ref.dtype)


def _moe_up(meta, xs3, w1, b1, n_blocks):
    n_slots = xs3.shape[0]
    nf = D_FF // MOE_TF
    grid_spec = pltpu.PrefetchScalarGridSpec(
        num_scalar_prefetch=len(meta),
        grid=(nf, n_blocks),
        in_specs=[
            pl.BlockSpec((MOE_TM, ROW_SUB, LANES), lambda f, i, be, bx, *_: (bx[i], 0, 0)),
            pl.BlockSpec(memory_space=pl.ANY),
            pl.BlockSpec((1, 1, MOE_TF), lambda f, i, be, bx, *_: (be[i], 0, f)),
            pl.BlockSpec((1, 1, MOE_TF), lambda f, i, be, bx, *_: (be[i], 0, nf + f)),
        ],
        out_specs=pl.BlockSpec((MOE_TM, MOE_TF), lambda f, i, *_: (i, f)),
        scratch_shapes=[
            pltpu.VMEM((2, 2, D_MODEL, MOE_TF), F32),
            pltpu.SemaphoreType.DMA((2, 2)),
        ],
    )
    return pl.pallas_call(
        _moe_up_kernel,
        grid_spec=grid_spec,
        out_shape=jax.ShapeDtypeStruct((n_slots, D_FF), BF16),
        compiler_params=_cparams(("arbitrary", "arbitrary")),
        name="moe_up",
    )(*meta, xs3, w1, b1, b1)


def _moe_down_kernel(be_ref, bx_ref, nr_ref, first_ref, seg_ref, nxt_ref, nseg_ref, a_ref, w_hbm, b_ref,
                     o_ref, wbuf_ref, sem):
    i = pl.program_id(0)
    nrows = nr_ref[i]

    def copies(e, slot):
        return [pltpu.make_async_copy(w_hbm.at[e], wbuf_ref.at[slot], sem.at[slot])]

    slot = _stream_expert_weights((be_ref, bx_ref, nr_ref, first_ref, seg_ref, nxt_ref, nseg_ref), i, copies)

    for cond, pieces, dead in _block_cases(nrows):
        @pl.when(cond)
        def _(pieces=pieces, dead=dead):
            for start, size in pieces:
                rows = slice(start, start + size)
                y = _dot_w(a_ref[rows, :], wbuf_ref[slot]) + b_ref[0]
                o_ref[rows] = y.astype(o_ref.dtype).reshape(size, ROW_SUB, LANES)
            if dead < MOE_TM:
                o_ref[dead:] = jnp.zeros((MOE_TM - dead, ROW_SUB, LANES), o_ref.dtype)


def _moe_down(meta, act, w2, b2, n_blocks):
    n_slots = act.shape[0]
    grid_spec = pltpu.PrefetchScalarGridSpec(
        num_scalar_prefetch=len(meta),
        grid=(n_blocks,),
        in_specs=[
            pl.BlockSpec((MOE_TM, D_FF), lambda i, be, bx, *_: (bx[i], 0)),
            pl.BlockSpec(memory_space=pl.ANY),
            pl.BlockSpec((1, 1, D_MODEL), lambda i, be, bx, *_: (be[i], 0, 0)),
        ],
        out_specs=pl.BlockSpec((MOE_TM, ROW_SUB, LANES), lambda i, *_: (i, 0, 0)),
        scratch_shapes=[
            pltpu.VMEM((2, D_FF, D_MODEL), F32),
            pltpu.SemaphoreType.DMA((2,)),
        ],
    )
    return pl.pallas_call(
        _moe_down_kernel,
        grid_spec=grid_spec,
        out_shape=jax.ShapeDtypeStruct((n_slots, ROW_SUB, LANES), BF16),
        compiler_params=_cparams(("arbitrary",)),
        name="moe_down",
    )(*meta, act, w2, b2)


def _combine_kernel(tab_ref, tabn_ref, pos_ref, gate_ref, y_ref, h_ref, o_ref, buf_ref, stage_ref, sem, *, tc):
    i = pl.program_id(0)
    slot = lax.rem(i, 2)

    def chunks(t_ref, s, start):
        def per_chunk(c, carry):
            cp = pltpu.make_async_copy(y_ref.at[pl.ds(t_ref[0, 0, 1 + c], COMBINE_CHUNK)],
                                       buf_ref.at[s, pl.ds(c * COMBINE_CHUNK, COMBINE_CHUNK)], sem.at[s])
            cp.start() if start else cp.wait()
            return carry

        lax.fori_loop(0, t_ref[0, 0, 0], per_chunk, 0)

    @pl.when(i == 0)
    def _():
        chunks(tab_ref, slot, True)

    @pl.when(i + 1 < pl.num_programs(0))
    def _():
        chunks(tabn_ref, 1 - slot, True)

    chunks(tab_ref, slot, False)

    def token(t, carry):
        acc = jnp.zeros((ROW_SUB, LANES), F32)
        for kk in range(TOP_K):
            j = t * TOP_K + kk
            acc = acc + gate_ref[0, 0, j] * buf_ref[slot, pos_ref[0, 0, j]].astype(F32)
        stage_ref[t] = acc
        return carry

    lax.fori_loop(0, tc, token, 0)
    o_ref[...] = h_ref[...] + stage_ref[...].reshape(tc, D_MODEL)


def _combine(tables, pos3, gate3, y3, h2, tc):
    t, d = h2.shape
    nt = t // tc
    n_tab = tables.shape[2]
    buf_rows = (n_tab - 1) * COMBINE_CHUNK
    smem = lambda shape, nxt: pl.BlockSpec(
        shape, (lambda i: (jnp.minimum(i + 1, nt - 1), 0, 0)) if nxt else (lambda i: (i, 0, 0)),
        memory_space=pltpu.SMEM)
    return pl.pallas_call(
        functools.partial(_combine_kernel, tc=tc),
        grid=(nt,),
        in_specs=[
            smem((1, 1, n_tab), False), smem((1, 1, n_tab), True),
            smem((1, 1, tc * TOP_K), False), smem((1, 1, tc * TOP_K), False),
            pl.BlockSpec(memory_space=pl.ANY),
            pl.BlockSpec((tc, d), lambda i: (i, 0)),
        ],
        out_specs=pl.BlockSpec((tc, d), lambda i: (i, 0)),
        out_shape=jax.ShapeDtypeStruct((t, d), F32),
        scratch_shapes=[
            pltpu.VMEM((2, buf_rows, ROW_SUB, LANES), BF16),
            pltpu.VMEM((tc, ROW_SUB, LANES), F32),
            pltpu.SemaphoreType.DMA((2,)),
        ],
        compiler_params=_cparams(("arbitrary",)),
        name="combine",
    )(tables, tables, pos3, gate3, y3, h2)


def _combine_plan(route, gates, slot_start, tc):
    t = route.shape[0]
    nt = t // tc
    idx = route[:, :TOP_K].reshape(nt, 1, tc * TOP_K)
    rank = route[:, TOP_K:2 * TOP_K].reshape(nt, tc * TOP_K)
    hit = idx == jnp.arange(N_EXPERTS, dtype=I32)[None, :, None]
    in_tile = jnp.sum(hit.astype(I32), axis=2)
    before = jnp.cumsum(in_tile, axis=0) - in_tile
    n_chunks = (in_tile + COMBINE_CHUNK - 1) // COMBINE_CHUNK
    c_end = jnp.cumsum(n_chunks, axis=1)
    c_start = c_end - n_chunks
    off = COMBINE_CHUNK * c_start
    pos = jnp.sum(jnp.where(hit, (off - before)[:, :, None], 0), axis=1) + rank
    max_chunks = (tc * TOP_K + N_EXPERTS * (COMBINE_CHUNK - 1)) // COMBINE_CHUNK
    c_ids = jnp.arange(max_chunks, dtype=I32)
    owner = jnp.logical_and(c_start[:, :, None] <= c_ids, c_ids < c_end[:, :, None])
    first = slot_start[None, :] + before - off
    src = jnp.sum(jnp.where(owner, first[:, :, None], 0), axis=1) + COMBINE_CHUNK * c_ids
    tables = jnp.concatenate([c_end[:, -1:], src], axis=1).astype(I32)
    gate3 = gates[:, :TOP_K].reshape(nt, 1, tc * TOP_K)
    return tables.reshape(nt, 1, 1 + max_chunks), pos.astype(I32).reshape(nt, 1, -1), gate3


def _moe_plan(counts, route, n_blocks):
    idx = route[:, :TOP_K].reshape(-1)
    rank = route[:, TOP_K:2 * TOP_K].reshape(-1)
    nblk = (counts + MOE_TM - 1) // MOE_TM
    blk_end = jnp.cumsum(nblk)
    blk_start = blk_end - nblk
    dest = (blk_start * MOE_TM)[idx] + rank
    total = blk_end[-1]
    ids = jnp.arange(n_blocks, dtype=I32)
    live = ids < total
    src = jnp.minimum(ids, total - 1)
    be = jnp.minimum(jnp.sum(blk_end[None, :] <= src[:, None], axis=1), N_EXPERTS - 1).astype(I32)
    nr = jnp.where(live, jnp.clip(counts[be] - (ids - blk_start[be]) * MOE_TM, 0, MOE_TM), 0).astype(I32)
    first = jnp.logical_and(live, ids == blk_start[be]).astype(I32)
    has = counts > 0
    ordinal = jnp.cumsum(has.astype(I32)) - 1
    e_ids = jnp.arange(N_EXPERTS, dtype=I32)
    later = jnp.logical_and(has[None, :], e_ids[None, :] > e_ids[:, None])
    nxt_e = jnp.min(jnp.where(later, e_ids[None, :], N_EXPERTS), axis=1)
    nxt_e = jnp.where(nxt_e == N_EXPERTS, jnp.argmax(has).astype(I32), nxt_e)
    nseg = jnp.sum(has.astype(I32)).reshape(1)
    meta = (be, src.astype(I32), nr, first, ordinal[be].astype(I32), nxt_e[be].astype(I32), nseg)
    return dest.astype(I32), meta, (blk_start * MOE_TM).astype(I32)


def _layer(h, mem, lam_init, norm_mix_g, w_in, conv_w, conv_b, dt_bias, a_log, d_skip, ssd_norm_g,
           da_q_norm_g, da_k_norm_g, lq1, lk1, lq2, lk2, da_subln_g, w_out, norm_xa_g, norm_mem_g,
           xa_wq, xa_wkv, xa_q_norm_g, xa_k_norm_g, xa_wo, norm_ffn_g, router_w, router_b,
           moe_w1, moe_b1, moe_w2, moe_b2):
    bsz, seq, d = h.shape
    t = bsz * seq
    mem_len = mem.shape[1]
    row = lambda v: v.astype(F32).reshape(1, -1)
    x2 = h.reshape(t, d)

    o1 = SSD_WIDTH + XBC_WIDTH
    o2 = o1 + SSD_HEADS
    w_main = _w_prep(w_in)
    w_dt = _split3_host(jnp.pad(w_in[:, o1:o2], ((0, 0), (0, LANES - SSD_HEADS))))
    proj, dt_raw = _in_proj(x2, row(norm_mix_g), w_main, w_dt)
    proj3 = proj.reshape(bsz, seq, PROJ_COLS)
    y_ssd = _ssd(proj3, dt_raw.reshape(bsz, seq, LANES), conv_w, conv_b, dt_bias, a_log, d_skip, ssd_norm_g)

    gq = row(jnp.tile(da_q_norm_g.astype(F32), 2) * (DA_HALF ** -0.5))
    gk = row(jnp.tile(da_k_norm_g.astype(F32), 2))
    lvec = jnp.stack([lq1, lk1, lq2, lk2]).astype(F32)
    y_da = _diff_attn(proj3, gq, gk, lvec, row(da_subln_g), lam_init)

    h1 = _mm_res([y_ssd.reshape(t, SSD_WIDTH), y_da.reshape(t, DA_WIDTH)], w_out.astype(BF16), x2)

    eq = jnp.tile(xa_q_norm_g.astype(F32) * (XA_HEAD_DIM ** -0.5), XA_HEADS).reshape(1, -1)
    qx = _norm_proj(h1, row(norm_xa_g), xa_wq.astype(BF16), eq[:, :XA_HEAD_DIM], XA_HEADS, 1024, XA_HEAD_DIM)
    kv = _norm_proj(mem.reshape(bsz * mem_len, d), row(norm_mem_g), xa_wkv.astype(BF16),
                    row(xa_k_norm_g), XA_HEADS, bsz * mem_len, XA_HEAD_DIM)
    ox = _xattn(qx, kv, seq, mem_len)
    h2 = _mm_res([ox], xa_wo.astype(BF16), h1)

    rw3 = _split3_host(jnp.pad(router_w, ((0, 0), (0, LANES - N_EXPERTS))))
    rb = jnp.pad(router_b.astype(F32), (0, LANES - N_EXPERTS), constant_values=-jnp.inf).reshape(1, LANES)
    hf, route, gates, cnt = _router(h2, row(norm_ffn_g), rw3, rb)
    n_blocks = (t * TOP_K) // MOE_TM + N_EXPERTS + 1
    n_slots = n_blocks * MOE_TM
    dest, meta, slot_start = _moe_plan(cnt[0, :N_EXPERTS].astype(I32), route, n_blocks)
    tt = 512
    xs = _dispatch(dest.reshape(t // tt, 1, tt * TOP_K), meta[2], hf, n_blocks, tt)
    act = _moe_up(meta, xs, moe_w1,
                  moe_b1.reshape(N_EXPERTS, 1, 2 * D_FF), n_blocks)
    y = _moe_down(meta, act, moe_w2, moe_b2.reshape(N_EXPERTS, 1, D_MODEL), n_blocks)
    tc = 256
    out = _combine(*_combine_plan(route, gates, slot_start, tc), y, h2, tc)
    return out.reshape(bsz, seq, d)


def kernel(x, mem, norm_mix_g, w_in, conv_w, conv_b, dt_bias, a_log, d_skip, ssd_norm_g, da_q_norm_g,
           da_k_norm_g, lambda_q1, lambda_k1, lambda_q2, lambda_k2, da_subln_g, w_out, norm_xa_g,
           norm_mem_g, xa_wq, xa_wkv, xa_q_norm_g, xa_k_norm_g, xa_wo, norm_ffn_g, router_w, router_b,
           moe_w1, moe_b1, moe_w2, moe_b2):
    h = x
    for layer in range(norm_mix_g.shape[0]):
        lam_init = 0.8 - 0.6 * math.exp(-0.3 * layer)
        h = _layer(h, mem, lam_init, norm_mix_g[layer], w_in[layer], conv_w[layer], conv_b[layer],
                   dt_bias[layer], a_log[layer], d_skip[layer], ssd_norm_g[layer], da_q_norm_g[layer],
                   da_k_norm_g[layer], lambda_q1[layer], lambda_k1[layer], lambda_q2[layer],
                   lambda_k2[layer], da_subln_g[layer], w_out[layer], norm_xa_g[layer], norm_mem_g[layer],
                   xa_wq[layer], xa_wkv[layer], xa_q_norm_g[layer], xa_k_norm_g[layer], xa_wo[layer],
                   norm_ffn_g[layer], router_w[layer], router_b[layer], moe_w1[layer], moe_b1[layer],
                   moe_w2[layer], moe_b2[layer])
    return h
```

```python
import functools
import math

import jax
import jax.numpy as jnp
from jax import lax
from jax.experimental import pallas as pl
from jax.experimental.pallas import tpu as pltpu

F32 = jnp.float32
BF16 = jnp.bfloat16
I32 = jnp.int32

D_MODEL = 2048
SSD_WIDTH = 1024
SSD_HEAD_DIM = 64
SSD_HEADS = 16
SSD_GROUPS = 2
SSD_STATE = 128
SSD_CONV = 4
SSD_CHUNK = 128
XBC_WIDTH = SSD_WIDTH + 2 * SSD_GROUPS * SSD_STATE
DA_WIDTH = 1024
DA_HEAD_DIM = 128
DA_HALF = 64
DA_HEADS = 8
XA_HEADS = 4
XA_HEAD_DIM = D_MODEL // XA_HEADS
N_EXPERTS = 32
TOP_K = 4
D_FF = 2048
SWIGLU_LIMIT = 7.0
SWIGLU_ALPHA = 1.702
EPS = 1e-6

LANES = 128
ROW_SUB = D_MODEL // LANES
VMEM_LIMIT = 56 * 1024 * 1024

PROJ_COLS = SSD_WIDTH + XBC_WIDTH + 3 * DA_WIDTH
COL_Q = (SSD_WIDTH + XBC_WIDTH) // LANES
COL_V = COL_Q + 2 * DA_HEADS

MOE_TM = 512
MOE_TF = 1024
COMBINE_CHUNK = 8


def _dot(a, b):
    return jnp.dot(a, b, preferred_element_type=F32)


def _dot_nt(a, b):
    return lax.dot_general(a, b, (((1,), (1,)), ((), ())), preferred_element_type=F32)


def _dot_tn(a, b):
    return lax.dot_general(a, b, (((0,), (0,)), ((), ())), preferred_element_type=F32)


def _rms(x):
    return x * lax.rsqrt(jnp.mean(x * x, axis=-1, keepdims=True) + EPS)


def _split2(x):
    hi = x.astype(BF16)
    mid = (x - hi.astype(F32)).astype(BF16)
    return hi, mid


def _split3(x):
    hi = x.astype(BF16)
    r = x - hi.astype(F32)
    mid = r.astype(BF16)
    lo = (r - mid.astype(F32)).astype(BF16)
    return hi, mid, lo


def _split3_host(w):
    hi, mid, lo = _split3(w.astype(F32))
    return jnp.stack([hi, mid, lo])


def _silu(x):
    return x * jax.nn.sigmoid(x)


def _cparams(sem):
    return pltpu.CompilerParams(dimension_semantics=sem, vmem_limit_bytes=VMEM_LIMIT)


def _cast_kernel(w_ref, o_ref):
    o_ref[...] = w_ref[...].astype(o_ref.dtype)


def _w_prep(w_t, tr=512):
    n, k = w_t.shape
    o1 = SSD_WIDTH + XBC_WIDTH
    src_row = lambda i: SSD_HEADS * ((tr // SSD_HEADS) * i + jnp.where(i < o1 // tr, 0, 1))
    return pl.pallas_call(
        _cast_kernel,
        grid=(PROJ_COLS // tr,),
        in_specs=[pl.BlockSpec((pl.Element(tr), pl.Element(k)), lambda i: (src_row(i), 0))],
        out_specs=pl.BlockSpec((tr, k), lambda i: (i, 0)),
        out_shape=jax.ShapeDtypeStruct((PROJ_COLS, k), BF16),
        compiler_params=_cparams(("parallel",)),
        name="w_prep",
    )(w_t)


def _w_dt_rows(w_t):
    k = w_t.shape[1]
    o1 = SSD_WIDTH + XBC_WIDTH
    return pl.pallas_call(
        _cast_kernel,
        grid=(1,),
        in_specs=[pl.BlockSpec((SSD_HEADS, k), lambda i: (o1 // SSD_HEADS, 0))],
        out_specs=pl.BlockSpec((SSD_HEADS, k), lambda i: (0, 0)),
        out_shape=jax.ShapeDtypeStruct((SSD_HEADS, k), F32),
        name="w_dt_rows",
    )(w_t)


def _in_proj_kernel(x_ref, g_ref, w_ref, wdt_ref, o_ref, dt_ref, xn_ref):
    @pl.when(pl.program_id(1) == 0)
    def _():
        hn = _rms(x_ref[...]) * g_ref[...]
        hi, mid = _split2(hn)
        xn_ref[...] = hi
        dt_ref[...] = _dot(hi, wdt_ref[0]) + (_dot(hi, wdt_ref[1]) + _dot(mid, wdt_ref[0]))

    o_ref[...] = _dot_nt(xn_ref[...], w_ref[...]).astype(o_ref.dtype)


def _in_proj(x2, g, w_t, wdt3, tm=1024, tn=512):
    m, k = x2.shape
    n = w_t.shape[0]
    return pl.pallas_call(
        _in_proj_kernel,
        grid=(m // tm, n // tn),
        in_specs=[
            pl.BlockSpec((tm, k), lambda i, j: (i, 0)),
            pl.BlockSpec((1, k), lambda i, j: (0, 0)),
            pl.BlockSpec((tn, k), lambda i, j: (j, 0)),
            pl.BlockSpec((3, k, LANES), lambda i, j: (0, 0, 0)),
        ],
        out_specs=[
            pl.BlockSpec((tm, tn), lambda i, j: (i, j)),
            pl.BlockSpec((tm, LANES), lambda i, j: (i, 0)),
        ],
        out_shape=[jax.ShapeDtypeStruct((m, n), BF16), jax.ShapeDtypeStruct((m, LANES), F32)],
        scratch_shapes=[pltpu.VMEM((tm, k), BF16)],
        compiler_params=_cparams(("parallel", "arbitrary")),
        name="in_proj",
    )(x2, g, w_t, wdt3)


def _norm_proj_kernel(x_ref, g_ref, w_ref, eg_ref, o_ref, xn_ref, *, epi_tiles):
    j = pl.program_id(1)

    @pl.when(j == 0)
    def _():
        xn_ref[...] = (_rms(x_ref[...]) * g_ref[...]).astype(BF16)

    acc = _dot(xn_ref[...], w_ref[...])

    @pl.when(j < epi_tiles)
    def _():
        o_ref[...] = (_rms(acc) * eg_ref[...]).astype(o_ref.dtype)

    @pl.when(j >= epi_tiles)
    def _():
        o_ref[...] = acc.astype(o_ref.dtype)


def _norm_proj(x2, g, w, eg, epi_tiles, tm, tn):
    m, k = x2.shape
    n = w.shape[1]
    return pl.pallas_call(
        functools.partial(_norm_proj_kernel, epi_tiles=epi_tiles),
        grid=(m // tm, n // tn),
        in_specs=[
            pl.BlockSpec((tm, k), lambda i, j: (i, 0)),
            pl.BlockSpec((1, k), lambda i, j: (0, 0)),
            pl.BlockSpec((k, tn), lambda i, j: (0, j)),
            pl.BlockSpec((1, tn), lambda i, j: (0, 0)),
        ],
        out_specs=pl.BlockSpec((tm, tn), lambda i, j: (i, j)),
        out_shape=jax.ShapeDtypeStruct((m, n), BF16),
        scratch_shapes=[pltpu.VMEM((tm, k), BF16)],
        compiler_params=_cparams(("parallel", "arbitrary")),
        name="norm_proj",
    )(x2, g, w, eg)


def _mm_res_kernel(*refs, n_a):
    a_refs, w_refs = refs[:n_a], refs[n_a:2 * n_a]
    res_ref, o_ref = refs[2 * n_a], refs[2 * n_a + 1]
    acc = res_ref[...]
    for a, w in zip(a_refs, w_refs):
        acc = acc + _dot(a[...], w[...])
    o_ref[...] = acc


def _mm_res(a_list, w, res, tm=1024, tn=1024):
    m, n = res.shape
    n_a = len(a_list)
    in_specs = []
    for a in a_list:
        in_specs.append(pl.BlockSpec((tm, a.shape[1]), lambda i, j: (i, 0)))
    for idx, a in enumerate(a_list):
        in_specs.append(pl.BlockSpec((a.shape[1], tn), lambda i, j, idx=idx: (idx, j)))
    in_specs.append(pl.BlockSpec((tm, tn), lambda i, j: (i, j)))
    return pl.pallas_call(
        functools.partial(_mm_res_kernel, n_a=n_a),
        grid=(m // tm, n // tn),
        in_specs=in_specs,
        out_specs=pl.BlockSpec((tm, tn), lambda i, j: (i, j)),
        out_shape=jax.ShapeDtypeStruct((m, n), F32),
        compiler_params=_cparams(("parallel", "arbitrary")),
        name="mm_res",
    )(*a_list, *([w] * n_a), res)


def _ssd_kernel(z_ref, xs_ref, b_ref, c_ref, dtr_ref, cw_ref, cb_ref, dtb_ref, alog_ref,
                dsk_ref, ng_ref, sel_ref, o_ref, ext_ref, ht_ref, y_ref, *, nb):
    L = SSD_CHUNK

    @pl.when(pl.program_id(1) == 0)
    def _():
        ht_ref[...] = jnp.zeros_like(ht_ref)
        ext_ref[:, 0:8, :] = jnp.zeros((nb, 8, XBC_WIDTH), F32)

    row = lax.broadcasted_iota(I32, (L, L), 0)
    col = lax.broadcasted_iota(I32, (L, L), 1)
    causal = col <= row
    tril = jnp.where(causal, 1.0, 0.0).astype(BF16)
    lane = lax.broadcasted_iota(I32, (L, LANES), 1)
    first = lane < SSD_HEAD_DIM
    a = -jnp.exp(alog_ref[...])

    for bb in range(nb):
        _ssd_chunk(bb, z_ref, xs_ref, b_ref, c_ref, dtr_ref, cw_ref, cb_ref, dtb_ref, a, dsk_ref, ng_ref,
                   sel_ref, o_ref, ext_ref, ht_ref, y_ref, causal, tril, first)


def _ssd_chunk(bb, z_ref, xs_ref, b_ref, c_ref, dtr_ref, cw_ref, cb_ref, dtb_ref, a, dsk_ref, ng_ref,
               sel_ref, o_ref, ext_ref, ht_ref, y_ref, causal, tril, first):
    L = SSD_CHUNK
    n_pairs = SSD_HEADS // 2
    pairs_per_group = n_pairs // SSD_GROUPS
    c_off = SSD_WIDTH + SSD_GROUPS * SSD_STATE
    ext = ext_ref.at[bb]
    ext[8:8 + L, 0:SSD_WIDTH] = xs_ref[bb].astype(F32)
    ext[8:8 + L, SSD_WIDTH:c_off] = b_ref[bb].astype(F32)
    ext[8:8 + L, c_off:XBC_WIDTH] = c_ref[bb].astype(F32)
    acc = jnp.broadcast_to(cb_ref[...], (L, XBC_WIDTH))
    for k in range(SSD_CONV):
        acc = acc + cw_ref[k:k + 1, :] * ext[pl.ds(8 - (SSD_CONV - 1) + k, L), :]
    ext[0:8, :] = ext[L:L + 8, :]
    xbc = _silu(acc)

    dt = jax.nn.softplus(dtr_ref[bb] + dtb_ref[...])
    adt = dt * a
    a_hi, a_mid, a_lo = _split3(adt)
    acum = _dot(tril, a_hi) + (_dot(tril, a_mid) + _dot(tril, a_lo))
    acum_t = acum.T
    pieces = [jnp.concatenate([p, q], axis=0) for p, q in zip(_split3(acum), _split3(dt))]

    cbs = []
    for g in range(SSD_GROUPS):
        bg = xbc[:, SSD_WIDTH + g * SSD_STATE:SSD_WIDTH + (g + 1) * SSD_STATE].astype(BF16)
        cg = xbc[:, c_off + g * SSD_STATE:c_off + (g + 1) * SSD_STATE].astype(BF16)
        cbs.append((bg, cg, _dot_nt(cg, bg)))

    for p in range(n_pairs):
        bg, cg, cb = cbs[p // pairs_per_group]
        sel = sel_ref[:, 2 * LANES * p:2 * LANES * (p + 1)]
        fb = _dot(pieces[0], sel) + (_dot(pieces[1], sel) + _dot(pieces[2], sel))
        ms = []
        for i in range(2):
            h = 2 * p + i
            seg = fb[:L, i * LANES:(i + 1) * LANES] - acum_t[h:h + 1, :]
            dec = jnp.exp(jnp.where(causal, seg, -jnp.inf))
            ms.append((cb * dec).astype(BF16))
        ac = jnp.where(first, fb[:L, :LANES], fb[:L, LANES:])
        dtp = jnp.where(first, fb[L:, :LANES], fb[L:, LANES:])
        xp = xbc[:, p * LANES:(p + 1) * LANES]
        xdt = xp * dtp
        y = _dot(ms[0], jnp.where(first, xdt, 0.0).astype(BF16))
        y = y + _dot(ms[1], jnp.where(first, 0.0, xdt).astype(BF16))
        ht = ht_ref[bb * n_pairs + p]
        y = y + _dot(cg, ht.astype(BF16)) * jnp.exp(ac)
        a_last = ac[L - 1:L, :]
        ht_ref[bb * n_pairs + p] = (ht * jnp.exp(a_last)
                                    + _dot_tn(bg, (xdt * jnp.exp(a_last - ac)).astype(BF16)))
        y_ref[bb, :, p * LANES:(p + 1) * LANES] = y + xp * dsk_ref[:, p * LANES:(p + 1) * LANES]

    yg = y_ref[bb] * _silu(z_ref[bb].astype(F32))
    gw = SSD_WIDTH // SSD_GROUPS
    for g in range(SSD_GROUPS):
        blk = _rms(yg[:, g * gw:(g + 1) * gw]) * ng_ref[:, g * gw:(g + 1) * gw]
        o_ref[bb, :, g * gw:(g + 1) * gw] = blk.astype(o_ref.dtype)


def _ssd(proj3, dt3, conv_w, conv_b, dt_bias, a_log, d_skip, norm_g, nb=2):
    bsz, s, _ = proj3.shape
    L = SSD_CHUNK
    pad = LANES - SSD_HEADS
    dtb = jnp.pad(dt_bias.astype(F32), (0, pad)).reshape(1, LANES)
    alog = jnp.pad(a_log.astype(F32), (0, pad)).reshape(1, LANES)
    dsk = jnp.repeat(d_skip.astype(F32), SSD_HEAD_DIM).reshape(1, SSD_WIDTH)
    sel = (jnp.arange(LANES)[:, None] == (jnp.arange(SSD_HEADS * LANES)[None, :] // LANES)).astype(BF16)
    const = lambda shape: pl.BlockSpec(shape, lambda b, c: (0,) * len(shape))
    return pl.pallas_call(
        functools.partial(_ssd_kernel, nb=nb),
        grid=(bsz // nb, s // L),
        in_specs=[
            pl.BlockSpec((nb, L, SSD_WIDTH), lambda b, c: (b, c, 0)),
            pl.BlockSpec((nb, L, SSD_WIDTH), lambda b, c: (b, c, 1)),
            pl.BlockSpec((nb, L, 2 * SSD_STATE), lambda b, c: (b, c, 2 * SSD_WIDTH // (2 * SSD_STATE))),
            pl.BlockSpec((nb, L, 2 * SSD_STATE), lambda b, c: (b, c, 2 * SSD_WIDTH // (2 * SSD_STATE) + 1)),
            pl.BlockSpec((nb, L, LANES), lambda b, c: (b, c, 0)),
            const((SSD_CONV, XBC_WIDTH)), const((1, XBC_WIDTH)), const((1, LANES)), const((1, LANES)),
            const((1, SSD_WIDTH)), const((1, SSD_WIDTH)), const((LANES, SSD_HEADS * LANES)),
        ],
        out_specs=pl.BlockSpec((nb, L, SSD_WIDTH), lambda b, c: (b, c, 0)),
        out_shape=jax.ShapeDtypeStruct((bsz, s, SSD_WIDTH), BF16),
        scratch_shapes=[
            pltpu.VMEM((nb, L + 8, XBC_WIDTH), F32),
            pltpu.VMEM((nb * SSD_HEADS // 2, SSD_STATE, LANES), F32),
            pltpu.VMEM((nb, L, SSD_WIDTH), F32),
        ],
        compiler_params=_cparams(("parallel", "arbitrary")),
        name="ssd",
    )(proj3, proj3, proj3, proj3, dt3, conv_w.astype(F32), conv_b.astype(F32).reshape(1, XBC_WIDTH),
      dtb, alog, dsk, norm_g.astype(F32).reshape(1, SSD_WIDTH), sel)


def _half_norm(x, g):
    ra = lax.broadcasted_iota(I32, (LANES, LANES), 0) // DA_HALF
    rb = lax.broadcasted_iota(I32, (LANES, LANES), 1) // DA_HALF
    grp = jnp.where(ra == rb, 1.0, 0.0).astype(BF16)
    hi, mid = _split2(x * x)
    ss = _dot(hi, grp) + _dot(mid, grp)
    return x * lax.rsqrt(ss * (1.0 / DA_HALF) + EPS) * g


def _da_kernel(q_ref, k_ref, v_ref, gq_ref, gk_ref, lv_ref, sg_ref, o_ref,
               kn_ref, va_ref, q2_ref, m_ref, l_ref, acc_ref, *, lam_init, tq, seq):
    qi = pl.program_id(2)
    tk = tq

    @pl.when(qi == 0)
    def _():
        for c in range(seq // tk):
            rows = slice(c * tk, (c + 1) * tk)
            kn_ref[rows, :] = _half_norm(k_ref[0, rows, :].astype(F32), gk_ref[...]).astype(BF16)
            va_ref[rows, :LANES] = v_ref[0, rows, :]
            va_ref[rows, LANES:] = jnp.ones((tk, LANES), BF16)

    q = _half_norm(q_ref[0].astype(F32), gq_ref[...]).astype(BF16)
    first = lax.broadcasted_iota(I32, q.shape, 1) < DA_HALF
    zero = jnp.zeros_like(q)
    q2_ref[0] = jnp.where(first, q, zero)
    q2_ref[1] = jnp.where(first, zero, q)
    m_ref[...] = jnp.full(m_ref.shape, -jnp.inf, F32)
    l_ref[...] = jnp.zeros_like(l_ref)
    acc_ref[...] = jnp.zeros_like(acc_ref)

    def step(c, masked):
        k = kn_ref[c * tk:(c + 1) * tk, :]
        va = va_ref[c * tk:(c + 1) * tk, :]
        if masked:
            keep = (lax.broadcasted_iota(I32, (tq, tk), 1) <= lax.broadcasted_iota(I32, (tq, tk), 0))
        for j in range(2):
            s = _dot_nt(q2_ref[j], k)
            if masked:
                s = jnp.where(keep, s, -jnp.inf)
            m_prev = m_ref[j]
            m_new = jnp.maximum(m_prev, jnp.max(s, axis=-1, keepdims=True))
            alpha = jnp.exp(m_prev - m_new)
            p = jnp.exp(s - jnp.concatenate([m_new] * (tk // LANES), axis=1))
            pv = _dot(p.astype(BF16), va)
            l_ref[j] = alpha * l_ref[j] + pv[:, LANES:]
            acc_ref[j] = alpha * acc_ref[j] + pv[:, :LANES]
            m_ref[j] = m_new

    for qv in range(seq // tq):
        @pl.when(qi == qv)
        def _(qv=qv):
            for c in range(qv):
                step(c, False)
            step(qv, True)

    lv = lv_ref[...]
    lam = (jnp.exp(jnp.sum(lv[0:1] * lv[1:2], axis=-1, keepdims=True))
           - jnp.exp(jnp.sum(lv[2:3] * lv[3:4], axis=-1, keepdims=True)) + lam_init)
    o = acc_ref[0] / l_ref[0] - lam * (acc_ref[1] / l_ref[1])
    o_ref[0] = (_rms(o) * (sg_ref[...] * (1.0 - lam_init))).astype(o_ref.dtype)


def _diff_attn(proj3, gq, gk, lvec, subln_g, lam_init, tq=512):
    bsz, s, _ = proj3.shape
    const = lambda shape: pl.BlockSpec(shape, lambda b, h, qi: (0,) * len(shape))
    return pl.pallas_call(
        functools.partial(_da_kernel, lam_init=lam_init, tq=tq, seq=s),
        grid=(bsz, DA_HEADS, s // tq),
        in_specs=[
            pl.BlockSpec((1, tq, LANES), lambda b, h, qi: (b, qi, COL_Q + h)),
            pl.BlockSpec((1, s, LANES), lambda b, h, qi: (b, 0, COL_Q + DA_HEADS + h)),
            pl.BlockSpec((1, s, LANES), lambda b, h, qi: (b, 0, COL_V + h)),
            const((1, LANES)), const((1, LANES)), const((4, DA_HALF)), const((1, LANES)),
        ],
        out_specs=pl.BlockSpec((1, tq, LANES), lambda b, h, qi: (b, qi, h)),
        out_shape=jax.ShapeDtypeStruct((bsz, s, DA_WIDTH), BF16),
        scratch_shapes=[
            pltpu.VMEM((s, LANES), BF16),
            pltpu.VMEM((s, 2 * LANES), BF16),
            pltpu.VMEM((2, tq, LANES), BF16),
            pltpu.VMEM((2, tq, LANES), F32),
            pltpu.VMEM((2, tq, LANES), F32),
            pltpu.VMEM((2, tq, LANES), F32),
        ],
        compiler_params=_cparams(("parallel", "parallel", "arbitrary")),
        name="diff_attn",
    )(proj3, proj3, proj3, gq, gk, lvec, subln_g)


def _xattn_kernel(q_ref, k_ref, v_ref, o_ref):
    s = _dot_nt(q_ref[...], k_ref[...])
    p = jnp.exp(s - jnp.max(s, axis=-1, keepdims=True))
    o = _dot(p.astype(BF16), v_ref[...]) / jnp.sum(p, axis=-1, keepdims=True)
    o_ref[...] = o.astype(o_ref.dtype)


def _xattn(qx, kv, seq, mem_len, tm=512):
    m = qx.shape[0]
    hd = XA_HEAD_DIM
    return pl.pallas_call(
        _xattn_kernel,
        grid=(m // tm, XA_HEADS),
        in_specs=[
            pl.BlockSpec((tm, hd), lambda i, j: (i, j)),
            pl.BlockSpec((mem_len, hd), lambda i, j: ((i * tm) // seq, j)),
            pl.BlockSpec((mem_len, hd), lambda i, j: ((i * tm) // seq, XA_HEADS + j)),
        ],
        out_specs=pl.BlockSpec((tm, hd), lambda i, j: (i, j)),
        out_shape=jax.ShapeDtypeStruct((m, XA_HEADS * hd), BF16),
        compiler_params=_cparams(("parallel", "parallel")),
        name="xattn",
    )(qx, kv, kv)


def _router_kernel(x_ref, g_ref, rw_ref, rb_ref, hf_ref, route_ref, gate_ref, cnt_ref, carry_ref, *, tm):
    i = pl.program_id(0)

    @pl.when(i == 0)
    def _():
        carry_ref[...] = jnp.zeros_like(carry_ref)

    hn = _rms(x_ref[...]) * g_ref[...]
    hi, mid = _split2(hn)
    hf_ref[...] = hi.reshape(tm, ROW_SUB, LANES)

    logits = _dot(hi, rw_ref[0]) + (_dot(hi, rw_ref[1]) + _dot(mid, rw_ref[0])) + rb_ref[...]
    lane = lax.broadcasted_iota(I32, (tm, LANES), 1)
    lane_f = lane.astype(F32)
    idxs, vals = [], []
    work = logits
    for _ in range(TOP_K):
        m = jnp.max(work, axis=-1, keepdims=True)
        ik = jnp.min(jnp.where(work == m, lane_f, float(LANES)), axis=-1, keepdims=True)
        idxs.append(ik)
        vals.append(m)
        work = jnp.where(lane_f == ik, -jnp.inf, work)
    es = [jnp.exp(v - vals[0]) for v in vals]
    den = es[0] + es[1] + es[2] + es[3]

    onehot = jnp.zeros((tm, LANES), F32)
    for ik in idxs:
        onehot = onehot + jnp.where(lane_f == ik, 1.0, 0.0)
    r = lax.broadcasted_iota(I32, (tm, tm), 0)
    c = lax.broadcasted_iota(I32, (tm, tm), 1)
    strict = jnp.where(c < r, 1.0, 0.0).astype(BF16)
    before = carry_ref[0:1, :] + _dot(strict, onehot.astype(BF16))
    carry_ref[0:1, :] = carry_ref[0:1, :] + jnp.sum(onehot, axis=0, keepdims=True)
    cnt_ref[...] = jnp.broadcast_to(carry_ref[0:1, :], cnt_ref.shape)

    route = jnp.zeros((tm, LANES), F32)
    gates = jnp.zeros((tm, LANES), F32)
    for kk in range(TOP_K):
        rank = jnp.sum(jnp.where(lane_f == idxs[kk], before, 0.0), axis=-1, keepdims=True)
        route = jnp.where(lane == kk, idxs[kk], route)
        route = jnp.where(lane == TOP_K + kk, rank, route)
        gates = jnp.where(lane == kk, es[kk] / den, gates)
    route_ref[...] = route.astype(I32)
    gate_ref[...] = gates


def _router(h2, g, rw3, rb, tm=512):
    t, d = h2.shape
    return pl.pallas_call(
        functools.partial(_router_kernel, tm=tm),
        grid=(t // tm,),
        in_specs=[
            pl.BlockSpec((tm, d), lambda i: (i, 0)),
            pl.BlockSpec((1, d), lambda i: (0, 0)),
            pl.BlockSpec((3, d, LANES), lambda i: (0, 0, 0)),
            pl.BlockSpec((1, LANES), lambda i: (0, 0)),
        ],
        out_specs=[
            pl.BlockSpec((tm, ROW_SUB, LANES), lambda i: (i, 0, 0)),
            pl.BlockSpec((tm, LANES), lambda i: (i, 0)),
            pl.BlockSpec((tm, LANES), lambda i: (i, 0)),
            pl.BlockSpec((8, LANES), lambda i: (0, 0)),
        ],
        out_shape=[
            jax.ShapeDtypeStruct((t, ROW_SUB, LANES), BF16),
            jax.ShapeDtypeStruct((t, LANES), I32),
            jax.ShapeDtypeStruct((t, LANES), F32),
            jax.ShapeDtypeStruct((8, LANES), F32),
        ],
        scratch_shapes=[pltpu.VMEM((8, LANES), F32)],
        compiler_params=_cparams(("arbitrary",)),
        name="router",
    )(h2, g, rw3, rb)


def _dispatch_kernel(dest_ref, nr_ref, hf_ref, xs_ref, zero_ref, sem, zsem, *, tt, n_blocks):
    @pl.when(pl.program_id(0) == 0)
    def _():
        zero_ref[...] = jnp.zeros_like(zero_ref)

        def fill(start):
            def body(b, carry):
                pad = MOE_TM - nr_ref[b]
                off = b * MOE_TM + nr_ref[b]
                size = MOE_TM
                while size >= 1:
                    hit = (pad & size) != 0
                    cp = pltpu.make_async_copy(zero_ref.at[pl.ds(0, size)], xs_ref.at[pl.ds(off, size)], zsem)

                    @pl.when(hit)
                    def _(cp=cp):
                        cp.start() if start else cp.wait()

                    off = off + jnp.where(hit, size, 0)
                    size //= 2
                return carry
            lax.fori_loop(0, n_blocks, body, 0)

        fill(True)
        fill(False)

    def copy(t, kk):
        return pltpu.make_async_copy(hf_ref.at[t], xs_ref.at[dest_ref[0, 0, t * TOP_K + kk]], sem)

    def issue(t, carry):
        for kk in range(TOP_K):
            copy(t, kk).start(priority=kk % 2)
        return carry

    def drain(t, carry):
        for kk in range(TOP_K):
            copy(t, kk).wait()
        return carry

    lax.fori_loop(0, tt, issue, 0)
    lax.fori_loop(0, tt, drain, 0)


def _dispatch(dest2, nr, hf3, n_blocks, tt):
    t = hf3.shape[0]
    return pl.pallas_call(
        functools.partial(_dispatch_kernel, tt=tt, n_blocks=n_blocks),
        grid=(t // tt,),
        in_specs=[
            pl.BlockSpec((1, 1, tt * TOP_K), lambda i: (i, 0, 0), memory_space=pltpu.SMEM),
            pl.BlockSpec(memory_space=pltpu.SMEM),
            pl.BlockSpec((tt, ROW_SUB, LANES), lambda i: (i, 0, 0)),
        ],
        out_specs=pl.BlockSpec(memory_space=pl.ANY),
        out_shape=jax.ShapeDtypeStruct((n_blocks * MOE_TM, ROW_SUB, LANES), BF16),
        scratch_shapes=[pltpu.VMEM((MOE_TM, ROW_SUB, LANES), BF16), pltpu.SemaphoreType.DMA(()),
                        pltpu.SemaphoreType.DMA(())],
        compiler_params=_cparams(("arbitrary",)),
        name="dispatch",
    )(dest2, nr, hf3)


def _stream_expert_weights(meta_refs, i, copies):
    be_ref, _, _, first_ref, seg_ref, nxt_ref, nseg_ref = meta_refs
    k = seg_ref[i]
    slot = lax.rem(k, 2)

    @pl.when(first_ref[i] == 1)
    def _():
        @pl.when(k == 0)
        def _():
            for c in copies(be_ref[i], slot):
                c.start()

        @pl.when(k < nseg_ref[0] - 1)
        def _():
            for c in copies(nxt_ref[i], 1 - slot):
                c.start()

        for c in copies(be_ref[i], slot):
            c.wait()

    return slot


def _dot_w(x, w):
    return lax.dot_general(x, w, (((1,), (0,)), ((), ())), preferred_element_type=F32)


def _block_cases(nrows):
    q = MOE_TM // 4
    return [
        (nrows > 3 * q, [(0, MOE_TM)], MOE_TM),
        (jnp.logical_and(nrows > 2 * q, nrows <= 3 * q), [(0, 2 * q), (2 * q, q)], 3 * q),
        (jnp.logical_and(nrows > q, nrows <= 2 * q), [(0, 2 * q)], 2 * q),
        (jnp.logical_and(nrows > 0, nrows <= q), [(0, q)], q),
        (nrows <= 0, [], 0),
    ]


def _moe_up_kernel(be_ref, bx_ref, nr_ref, first_ref, seg_ref, nxt_ref, nseg_ref, x_ref, w_hbm, bg_ref,
                   bu_ref, o_ref, wbuf_ref, sem):
    f = pl.program_id(0)
    i = pl.program_id(1)
    nrows = nr_ref[i]
    nf = D_FF // MOE_TF

    def copies(e, slot):
        out = []
        for part in range(2):
            col = pl.multiple_of((part * nf + f) * MOE_TF, MOE_TF)
            out.append(pltpu.make_async_copy(w_hbm.at[e, :, pl.ds(col, MOE_TF)], wbuf_ref.at[slot, part],
                                             sem.at[slot, part]))
        return out

    slot = _stream_expert_weights((be_ref, bx_ref, nr_ref, first_ref, seg_ref, nxt_ref, nseg_ref), i, copies)

    for cond, pieces, dead in _block_cases(nrows):
        @pl.when(cond)
        def _(pieces=pieces, dead=dead):
            for start, size in pieces:
                rows = slice(start, start + size)
                xb = x_ref[rows].reshape(size, D_MODEL)
                g = jnp.minimum(_dot_w(xb, wbuf_ref[slot, 0]) + bg_ref[0], SWIGLU_LIMIT)
                u = jnp.clip(_dot_w(xb, wbuf_ref[slot, 1]) + bu_ref[0], -SWIGLU_LIMIT, SWIGLU_LIMIT)
                act = g * jax.nn.sigmoid(SWIGLU_ALPHA * g) * (u + 1.0)
                o_ref[rows, :] = act.astype(o_ref.dtype)
            if dead < MOE_TM:
                o_ref[dead:, :] = jnp.zeros((MOE_TM - dead, MOE_TF), o_ref.dtype)


def _moe_up(meta, xs3, w1, b1, n_blocks):
    n_slots = xs3.shape[0]
    nf = D_FF // MOE_TF
    grid_spec = pltpu.PrefetchScalarGridSpec(
        num_scalar_prefetch=len(meta),
        grid=(nf, n_blocks),
        in_specs=[
            pl.BlockSpec((MOE_TM, ROW_SUB, LANES), lambda f, i, be, bx, *_: (bx[i], 0, 0)),
            pl.BlockSpec(memory_space=pl.ANY),
            pl.BlockSpec((1, 1, MOE_TF), lambda f, i, be, bx, *_: (be[i], 0, f)),
            pl.BlockSpec((1, 1, MOE_TF), lambda f, i, be, bx, *_: (be[i], 0, nf + f)),
        ],
        out_specs=pl.BlockSpec((MOE_TM, MOE_TF), lambda f, i, *_: (i, f)),
        scratch_shapes=[
            pltpu.VMEM((2, 2, D_MODEL, MOE_TF), F32),
            pltpu.SemaphoreType.DMA((2, 2)),
        ],
    )
    return pl.pallas_call(
        _moe_up_kernel,
        grid_spec=grid_spec,
        out_shape=jax.ShapeDtypeStruct((n_slots, D_FF), BF16),
        compiler_params=_cparams(("arbitrary", "arbitrary")),
        name="moe_up",
    )(*meta, xs3, w1, b1, b1)


def _moe_down_kernel(be_ref, bx_ref, nr_ref, first_ref, seg_ref, nxt_ref, nseg_ref, a_ref, w_hbm, b_ref,
                     o_ref, wbuf_ref, sem):
    i = pl.program_id(0)
    nrows = nr_ref[i]

    def copies(e, slot):
        return [pltpu.make_async_copy(w_hbm.at[e], wbuf_ref.at[slot], sem.at[slot])]

    slot = _stream_expert_weights((be_ref, bx_ref, nr_ref, first_ref, seg_ref, nxt_ref, nseg_ref), i, copies)

    for cond, pieces, dead in _block_cases(nrows):
        @pl.when(cond)
        def _(pieces=pieces, dead=dead):
            for start, size in pieces:
                rows = slice(start, start + size)
                y = _dot_w(a_ref[rows, :], wbuf_ref[slot]) + b_ref[0]
                o_ref[rows] = y.astype(o_ref.dtype).reshape(size, ROW_SUB, LANES)
            if dead < MOE_TM:
                o_ref[dead:] = jnp.zeros((MOE_TM - dead, ROW_SUB, LANES), o_ref.dtype)


def _moe_down(meta, act, w2, b2, n_blocks):
    n_slots = act.shape[0]
    grid_spec = pltpu.PrefetchScalarGridSpec(
        num_scalar_prefetch=len(meta),
        grid=(n_blocks,),
        in_specs=[
            pl.BlockSpec((MOE_TM, D_FF), lambda i, be, bx, *_: (bx[i], 0)),
            pl.BlockSpec(memory_space=pl.ANY),
            pl.BlockSpec((1, 1, D_MODEL), lambda i, be, bx, *_: (be[i], 0, 0)),
        ],
        out_specs=pl.BlockSpec((MOE_TM, ROW_SUB, LANES), lambda i, *_: (i, 0, 0)),
        scratch_shapes=[
            pltpu.VMEM((2, D_FF, D_MODEL), F32),
            pltpu.SemaphoreType.DMA((2,)),
        ],
    )
    return pl.pallas_call(
        _moe_down_kernel,
        grid_spec=grid_spec,
        out_shape=jax.ShapeDtypeStruct((n_slots, ROW_SUB, LANES), BF16),
        compiler_params=_cparams(("arbitrary",)),
        name="moe_down",
    )(*meta, act, w2, b2)


def _combine_kernel(tab_ref, tabn_ref, pos_ref, gate_ref, y_ref, h_ref, o_ref, buf_ref, stage_ref, sem, *, tc):
    i = pl.program_id(0)
    slot = lax.rem(i, 2)

    def chunks(t_ref, s, start):
        def per_chunk(c, carry):
            cp = pltpu.make_async_copy(y_ref.at[pl.ds(t_ref[0, 0, 1 + c], COMBINE_CHUNK)],
                                       buf_ref.at[s, pl.ds(c * COMBINE_CHUNK, COMBINE_CHUNK)], sem.at[s])
            cp.start() if start else cp.wait()
            return carry

        lax.fori_loop(0, t_ref[0, 0, 0], per_chunk, 0)

    @pl.when(i == 0)
    def _():
        chunks(tab_ref, slot, True)

    @pl.when(i + 1 < pl.num_programs(0))
    def _():
        chunks(tabn_ref, 1 - slot, True)

    chunks(tab_ref, slot, False)

    def token(t, carry):
        acc = jnp.zeros((ROW_SUB, LANES), F32)
        for kk in range(TOP_K):
            j = t * TOP_K + kk
            acc = acc + gate_ref[0, 0, j] * buf_ref[slot, pos_ref[0, 0, j]].astype(F32)
        stage_ref[t] = acc
        return carry

    lax.fori_loop(0, tc, token, 0)
    o_ref[...] = h_ref[...] + stage_ref[...].reshape(tc, D_MODEL)


def _combine(tables, pos3, gate3, y3, h2, tc):
    t, d = h2.shape
    nt = t // tc
    n_tab = tables.shape[2]
    buf_rows = (n_tab - 1) * COMBINE_CHUNK
    smem = lambda shape, nxt: pl.BlockSpec(
        shape, (lambda i: (jnp.minimum(i + 1, nt - 1), 0, 0)) if nxt else (lambda i: (i, 0, 0)),
        memory_space=pltpu.SMEM)
    return pl.pallas_call(
        functools.partial(_combine_kernel, tc=tc),
        grid=(nt,),
        in_specs=[
            smem((1, 1, n_tab), False), smem((1, 1, n_tab), True),
            smem((1, 1, tc * TOP_K), False), smem((1, 1, tc * TOP_K), False),
            pl.BlockSpec(memory_space=pl.ANY),
            pl.BlockSpec((tc, d), lambda i: (i, 0)),
        ],
        out_specs=pl.BlockSpec((tc, d), lambda i: (i, 0)),
        out_shape=jax.ShapeDtypeStruct((t, d), F32),
        scratch_shapes=[
            pltpu.VMEM((2, buf_rows, ROW_SUB, LANES), BF16),
            pltpu.VMEM((tc, ROW_SUB, LANES), F32),
            pltpu.SemaphoreType.DMA((2,)),
        ],
        compiler_params=_cparams(("arbitrary",)),
        name="combine",
    )(tables, tables, pos3, gate3, y3, h2)


def _combine_plan(route, gates, slot_start, tc):
    t = route.shape[0]
    nt = t // tc
    idx = route[:, :TOP_K].reshape(nt, 1, tc * TOP_K)
    rank = route[:, TOP_K:2 * TOP_K].reshape(nt, tc * TOP_K)
    hit = idx == jnp.arange(N_EXPERTS, dtype=I32)[None, :, None]
    in_tile = jnp.sum(hit.astype(I32), axis=2)
    before = jnp.cumsum(in_tile, axis=0) - in_tile
    n_chunks = (in_tile + COMBINE_CHUNK - 1) // COMBINE_CHUNK
    c_end = jnp.cumsum(n_chunks, axis=1)
    c_start = c_end - n_chunks
    off = COMBINE_CHUNK * c_start
    pos = jnp.sum(jnp.where(hit, (off - before)[:, :, None], 0), axis=1) + rank
    max_chunks = (tc * TOP_K + N_EXPERTS * (COMBINE_CHUNK - 1)) // COMBINE_CHUNK
    c_ids = jnp.arange(max_chunks, dtype=I32)
    owner = jnp.logical_and(c_start[:, :, None] <= c_ids, c_ids < c_end[:, :, None])
    first = slot_start[None, :] + before - off
    src = jnp.sum(jnp.where(owner, first[:, :, None], 0), axis=1) + COMBINE_CHUNK * c_ids
    tables = jnp.concatenate([c_end[:, -1:], src], axis=1).astype(I32)
    gate3 = gates[:, :TOP_K].reshape(nt, 1, tc * TOP_K)
    return tables.reshape(nt, 1, 1 + max_chunks), pos.astype(I32).reshape(nt, 1, -1), gate3


def _moe_plan(counts, route, n_blocks):
    idx = route[:, :TOP_K].reshape(-1)
    rank = route[:, TOP_K:2 * TOP_K].reshape(-1)
    nblk = (counts + MOE_TM - 1) // MOE_TM
    blk_end = jnp.cumsum(nblk)
    blk_start = blk_end - nblk
    dest = (blk_start * MOE_TM)[idx] + rank
    total = blk_end[-1]
    ids = jnp.arange(n_blocks, dtype=I32)
    live = ids < total
    src = jnp.minimum(ids, total - 1)
    be = jnp.minimum(jnp.sum(blk_end[None, :] <= src[:, None], axis=1), N_EXPERTS - 1).astype(I32)
    nr = jnp.where(live, jnp.clip(counts[be] - (ids - blk_start[be]) * MOE_TM, 0, MOE_TM), 0).astype(I32)
    first = jnp.logical_and(live, ids == blk_start[be]).astype(I32)
    has = counts > 0
    ordinal = jnp.cumsum(has.astype(I32)) - 1
    e_ids = jnp.arange(N_EXPERTS, dtype=I32)
    later = jnp.logical_and(has[None, :], e_ids[None, :] > e_ids[:, None])
    nxt_e = jnp.min(jnp.where(later, e_ids[None, :], N_EXPERTS), axis=1)
    nxt_e = jnp.where(nxt_e == N_EXPERTS, jnp.argmax(has).astype(I32), nxt_e)
    nseg = jnp.sum(has.astype(I32)).reshape(1)
    meta = (be, src.astype(I32), nr, first, ordinal[be].astype(I32), nxt_e[be].astype(I32), nseg)
    return dest.astype(I32), meta, (blk_start * MOE_TM).astype(I32)


def _layer(h, mem, lam_init, norm_mix_g, w_in, conv_w, conv_b, dt_bias, a_log, d_skip, ssd_norm_g,
           da_q_norm_g, da_k_norm_g, lq1, lk1, lq2, lk2, da_subln_g, w_out, norm_xa_g, norm_mem_g,
           xa_wq, xa_wkv, xa_q_norm_g, xa_k_norm_g, xa_wo, norm_ffn_g, router_w, router_b,
           moe_w1, moe_b1, moe_w2, moe_b2):
    bsz, seq, d = h.shape
    t = bsz * seq
    mem_len = mem.shape[1]
    row = lambda v: v.astype(F32).reshape(1, -1)
    x2 = h.reshape(t, d)

    w_t = jnp.swapaxes(w_in, 0, 1)
    w_main = _w_prep(w_t)
    w_dt = _split3_host(jnp.pad(jnp.swapaxes(_w_dt_rows(w_t), 0, 1), ((0, 0), (0, LANES - SSD_HEADS))))
    proj, dt_raw = _in_proj(x2, row(norm_mix_g), w_main, w_dt)
    proj3 = proj.reshape(bsz, seq, PROJ_COLS)
    y_ssd = _ssd(proj3, dt_raw.reshape(bsz, seq, LANES), conv_w, conv_b, dt_bias, a_log, d_skip, ssd_norm_g)

    gq = row(jnp.tile(da_q_norm_g.astype(F32), 2) * (DA_HALF ** -0.5))
    gk = row(jnp.tile(da_k_norm_g.astype(F32), 2))
    lvec = jnp.stack([lq1, lk1, lq2, lk2]).astype(F32)
    y_da = _diff_attn(proj3, gq, gk, lvec, row(da_subln_g), lam_init)

    h1 = _mm_res([y_ssd.reshape(t, SSD_WIDTH), y_da.reshape(t, DA_WIDTH)], w_out.astype(BF16), x2)

    eq = jnp.tile(xa_q_norm_g.astype(F32) * (XA_HEAD_DIM ** -0.5), XA_HEADS).reshape(1, -1)
    qx = _norm_proj(h1, row(norm_xa_g), xa_wq.astype(BF16), eq[:, :XA_HEAD_DIM], XA_HEADS, 1024, XA_HEAD_DIM)
    kv = _norm_proj(mem.reshape(bsz * mem_len, d), row(norm_mem_g), xa_wkv.astype(BF16),
                    row(xa_k_norm_g), XA_HEADS, bsz * mem_len, XA_HEAD_DIM)
    ox = _xattn(qx, kv, seq, mem_len)
    h2 = _mm_res([ox], xa_wo.astype(BF16), h1)

    rw3 = _split3_host(jnp.pad(router_w, ((0, 0), (0, LANES - N_EXPERTS))))
    rb = jnp.pad(router_b.astype(F32), (0, LANES - N_EXPERTS), constant_values=-jnp.inf).reshape(1, LANES)
    hf, route, gates, cnt = _router(h2, row(norm_ffn_g), rw3, rb)
    n_blocks = (t * TOP_K) // MOE_TM + N_EXPERTS + 1
    n_slots = n_blocks * MOE_TM
    dest, meta, slot_start = _moe_plan(cnt[0, :N_EXPERTS].astype(I32), route, n_blocks)
    tt = 512
    xs = _dispatch(dest.reshape(t // tt, 1, tt * TOP_K), meta[2], hf, n_blocks, tt)
    act = _moe_up(meta, xs, moe_w1,
                  moe_b1.reshape(N_EXPERTS, 1, 2 * D_FF), n_blocks)
    y = _moe_down(meta, act, moe_w2, moe_b2.reshape(N_EXPERTS, 1, D_MODEL), n_blocks)
    tc = 256
    out = _combine(*_combine_plan(route, gates, slot_start, tc), y, h2, tc)
    return out.reshape(bsz, seq, d)


def kernel(x, mem, norm_mix_g, w_in, conv_w, conv_b, dt_bias, a_log, d_skip, ssd_norm_g, da_q_norm_g,
           da_k_norm_g, lambda_q1, lambda_k1, lambda_q2, lambda_k2, da_subln_g, w_out, norm_xa_g,
           norm_mem_g, xa_wq, xa_wkv, xa_q_norm_g, xa_k_norm_g, xa_wo, norm_ffn_g, router_w, router_b,
           moe_w1, moe_b1, moe_w2, moe_b2):
    h = x
    for layer in range(norm_mix_g.shape[0]):
        lam_init = 0.8 - 0.6 * math.exp(-0.3 * layer)
        h = _layer(h, mem, lam_init, norm_mix_g[layer], w_in[layer], conv_w[layer], conv_b[layer],
                   dt_bias[layer], a_log[layer], d_skip[layer], ssd_norm_g[layer], da_q_norm_g[layer],
                   da_k_norm_g[layer], lambda_q1[layer], lambda_k1[layer], lambda_q2[layer],
                   lambda_k2[layer], da_subln_g[layer], w_out[layer], norm_xa_g[layer], norm_mem_g[layer],
                   xa_wq[layer], xa_wkv[layer], xa_q_norm_g[layer], xa_k_norm_g[layer], xa_wo[layer],
                   norm_ffn_g[layer], router_w[layer], router_b[layer], moe_w1[layer], moe_b1[layer],
                   moe_w2[layer], moe_b2[layer])
    return h
```

```python
import functools
import math

import jax
import jax.numpy as jnp
from jax import lax
from jax.experimental import pallas as pl
from jax.experimental.pallas import tpu as pltpu

F32 = jnp.float32
BF16 = jnp.bfloat16
I32 = jnp.int32

D_MODEL = 2048
SSD_WIDTH = 1024
SSD_HEAD_DIM = 64
SSD_HEADS = 16
SSD_GROUPS = 2
SSD_STATE = 128
SSD_CONV = 4
SSD_CHUNK = 128
XBC_WIDTH = SSD_WIDTH + 2 * SSD_GROUPS * SSD_STATE
DA_WIDTH = 1024
DA_HEAD_DIM = 128
DA_HALF = 64
DA_HEADS = 8
XA_HEADS = 4
XA_HEAD_DIM = D_MODEL // XA_HEADS
N_EXPERTS = 32
TOP_K = 4
D_FF = 2048
SWIGLU_LIMIT = 7.0
SWIGLU_ALPHA = 1.702
EPS = 1e-6

LANES = 128
ROW_SUB = D_MODEL // LANES
VMEM_LIMIT = 56 * 1024 * 1024

PROJ_COLS = SSD_WIDTH + XBC_WIDTH + 3 * DA_WIDTH
COL_Q = (SSD_WIDTH + XBC_WIDTH) // LANES
COL_V = COL_Q + 2 * DA_HEADS

MOE_TM = 512
MOE_TF = 1024
COMBINE_CHUNK = 16


def _dot(a, b):
    return jnp.dot(a, b, preferred_element_type=F32)


def _dot_nt(a, b):
    return lax.dot_general(a, b, (((1,), (1,)), ((), ())), preferred_element_type=F32)


def _dot_tn(a, b):
    return lax.dot_general(a, b, (((0,), (0,)), ((), ())), preferred_element_type=F32)


def _rms(x):
    return x * lax.rsqrt(jnp.mean(x * x, axis=-1, keepdims=True) + EPS)


def _split2(x):
    hi = x.astype(BF16)
    mid = (x - hi.astype(F32)).astype(BF16)
    return hi, mid


def _split3(x):
    hi = x.astype(BF16)
    r = x - hi.astype(F32)
    mid = r.astype(BF16)
    lo = (r - mid.astype(F32)).astype(BF16)
    return hi, mid, lo


def _split3_host(w):
    hi, mid, lo = _split3(w.astype(F32))
    return jnp.stack([hi, mid, lo])


def _silu(x):
    return x * jax.nn.sigmoid(x)


def _cparams(sem):
    return pltpu.CompilerParams(dimension_semantics=sem, vmem_limit_bytes=VMEM_LIMIT)


def _cast_kernel(w_ref, o_ref):
    o_ref[...] = w_ref[...].astype(o_ref.dtype)


def _w_prep(w_t, tr=512):
    n, k = w_t.shape
    o1 = SSD_WIDTH + XBC_WIDTH
    src_row = lambda i: SSD_HEADS * ((tr // SSD_HEADS) * i + jnp.where(i < o1 // tr, 0, 1))
    return pl.pallas_call(
        _cast_kernel,
        grid=(PROJ_COLS // tr,),
        in_specs=[pl.BlockSpec((pl.Element(tr), pl.Element(k)), lambda i: (src_row(i), 0))],
        out_specs=pl.BlockSpec((tr, k), lambda i: (i, 0)),
        out_shape=jax.ShapeDtypeStruct((PROJ_COLS, k), BF16),
        compiler_params=_cparams(("parallel",)),
        name="w_prep",
    )(w_t)


def _w_dt_rows(w_t):
    k = w_t.shape[1]
    o1 = SSD_WIDTH + XBC_WIDTH
    return pl.pallas_call(
        _cast_kernel,
        grid=(1,),
        in_specs=[pl.BlockSpec((SSD_HEADS, k), lambda i: (o1 // SSD_HEADS, 0))],
        out_specs=pl.BlockSpec((SSD_HEADS, k), lambda i: (0, 0)),
        out_shape=jax.ShapeDtypeStruct((SSD_HEADS, k), F32),
        name="w_dt_rows",
    )(w_t)


def _in_proj_kernel(x_ref, g_ref, w_ref, wdt_ref, o_ref, dt_ref, xn_ref):
    @pl.when(pl.program_id(1) == 0)
    def _():
        hn = _rms(x_ref[...]) * g_ref[...]
        hi, mid = _split2(hn)
        xn_ref[...] = hi
        dt_ref[...] = _dot(hi, wdt_ref[0]) + (_dot(hi, wdt_ref[1]) + _dot(mid, wdt_ref[0]))

    o_ref[...] = _dot_nt(xn_ref[...], w_ref[...]).astype(o_ref.dtype)


def _in_proj(x2, g, w_t, wdt3, tm=1024, tn=512):
    m, k = x2.shape
    n = w_t.shape[0]
    return pl.pallas_call(
        _in_proj_kernel,
        grid=(m // tm, n // tn),
        in_specs=[
            pl.BlockSpec((tm, k), lambda i, j: (i, 0)),
            pl.BlockSpec((1, k), lambda i, j: (0, 0)),
            pl.BlockSpec((tn, k), lambda i, j: (j, 0)),
            pl.BlockSpec((3, k, LANES), lambda i, j: (0, 0, 0)),
        ],
        out_specs=[
            pl.BlockSpec((tm, tn), lambda i, j: (i, j)),
            pl.BlockSpec((tm, LANES), lambda i, j: (i, 0)),
        ],
        out_shape=[jax.ShapeDtypeStruct((m, n), BF16), jax.ShapeDtypeStruct((m, LANES), F32)],
        scratch_shapes=[pltpu.VMEM((tm, k), BF16)],
        compiler_params=_cparams(("parallel", "arbitrary")),
        name="in_proj",
    )(x2, g, w_t, wdt3)


def _norm_proj_kernel(x_ref, g_ref, w_ref, eg_ref, o_ref, xn_ref, *, epi_tiles):
    j = pl.program_id(1)

    @pl.when(j == 0)
    def _():
        xn_ref[...] = (_rms(x_ref[...]) * g_ref[...]).astype(BF16)

    acc = _dot(xn_ref[...], w_ref[...])

    @pl.when(j < epi_tiles)
    def _():
        o_ref[...] = (_rms(acc) * eg_ref[...]).astype(o_ref.dtype)

    @pl.when(j >= epi_tiles)
    def _():
        o_ref[...] = acc.astype(o_ref.dtype)


def _norm_proj(x2, g, w, eg, epi_tiles, tm, tn):
    m, k = x2.shape
    n = w.shape[1]
    return pl.pallas_call(
        functools.partial(_norm_proj_kernel, epi_tiles=epi_tiles),
        grid=(m // tm, n // tn),
        in_specs=[
            pl.BlockSpec((tm, k), lambda i, j: (i, 0)),
            pl.BlockSpec((1, k), lambda i, j: (0, 0)),
            pl.BlockSpec((k, tn), lambda i, j: (0, j)),
            pl.BlockSpec((1, tn), lambda i, j: (0, 0)),
        ],
        out_specs=pl.BlockSpec((tm, tn), lambda i, j: (i, j)),
        out_shape=jax.ShapeDtypeStruct((m, n), BF16),
        scratch_shapes=[pltpu.VMEM((tm, k), BF16)],
        compiler_params=_cparams(("parallel", "arbitrary")),
        name="norm_proj",
    )(x2, g, w, eg)


def _mm_res_kernel(*refs, n_a):
    a_refs, w_refs = refs[:n_a], refs[n_a:2 * n_a]
    res_ref, o_ref = refs[2 * n_a], refs[2 * n_a + 1]
    acc = res_ref[...]
    for a, w in zip(a_refs, w_refs):
        acc = acc + _dot_w(a[...], w[...])
    o_ref[...] = acc


def _mm_res(a_list, w, res, tm=1024, tn=1024):
    m, n = res.shape
    n_a = len(a_list)
    in_specs = []
    for a in a_list:
        in_specs.append(pl.BlockSpec((tm, a.shape[1]), lambda j, i: (i, 0)))
    for idx, a in enumerate(a_list):
        in_specs.append(pl.BlockSpec((a.shape[1], tn), lambda j, i, idx=idx: (idx, j)))
    in_specs.append(pl.BlockSpec((tm, tn), lambda j, i: (i, j)))
    return pl.pallas_call(
        functools.partial(_mm_res_kernel, n_a=n_a),
        grid=(n // tn, m // tm),
        in_specs=in_specs,
        out_specs=pl.BlockSpec((tm, tn), lambda j, i: (i, j)),
        out_shape=jax.ShapeDtypeStruct((m, n), F32),
        compiler_params=_cparams(("parallel", "arbitrary")),
        name="mm_res",
    )(*a_list, *([w] * n_a), res)


def _ssd_kernel(z_ref, xs_ref, b_ref, c_ref, dtr_ref, cw_ref, cb_ref, dtb_ref, alog_ref,
                dsk_ref, ng_ref, sel_ref, o_ref, ext_ref, ht_ref, y_ref, *, nb):
    L = SSD_CHUNK

    @pl.when(pl.program_id(1) == 0)
    def _():
        ht_ref[...] = jnp.zeros_like(ht_ref)
        ext_ref[:, 0:8, :] = jnp.zeros((nb, 8, XBC_WIDTH), F32)

    row = lax.broadcasted_iota(I32, (L, L), 0)
    col = lax.broadcasted_iota(I32, (L, L), 1)
    causal = col <= row
    tril = jnp.where(causal, 1.0, 0.0).astype(BF16)
    lane = lax.broadcasted_iota(I32, (L, LANES), 1)
    first = lane < SSD_HEAD_DIM
    a = -jnp.exp(alog_ref[...])

    for bb in range(nb):
        _ssd_chunk(bb, z_ref, xs_ref, b_ref, c_ref, dtr_ref, cw_ref, cb_ref, dtb_ref, a, dsk_ref, ng_ref,
                   sel_ref, o_ref, ext_ref, ht_ref, y_ref, causal, tril, first)


def _ssd_chunk(bb, z_ref, xs_ref, b_ref, c_ref, dtr_ref, cw_ref, cb_ref, dtb_ref, a, dsk_ref, ng_ref,
               sel_ref, o_ref, ext_ref, ht_ref, y_ref, causal, tril, first):
    L = SSD_CHUNK
    n_pairs = SSD_HEADS // 2
    pairs_per_group = n_pairs // SSD_GROUPS
    c_off = SSD_WIDTH + SSD_GROUPS * SSD_STATE
    ext = ext_ref.at[bb]
    ext[8:8 + L, 0:SSD_WIDTH] = xs_ref[bb].astype(F32)
    ext[8:8 + L, SSD_WIDTH:c_off] = b_ref[bb].astype(F32)
    ext[8:8 + L, c_off:XBC_WIDTH] = c_ref[bb].astype(F32)
    acc = jnp.broadcast_to(cb_ref[...], (L, XBC_WIDTH))
    for k in range(SSD_CONV):
        acc = acc + cw_ref[k:k + 1, :] * ext[pl.ds(8 - (SSD_CONV - 1) + k, L), :]
    ext[0:8, :] = ext[L:L + 8, :]
    xbc = _silu(acc)

    dt = jax.nn.softplus(dtr_ref[bb] + dtb_ref[...])
    adt = dt * a
    a_hi, a_mid, a_lo = _split3(adt)
    acum = _dot(tril, a_hi) + (_dot(tril, a_mid) + _dot(tril, a_lo))
    acum_t = acum.T
    pieces = [jnp.concatenate([p, q], axis=0) for p, q in zip(_split3(acum), _split3(dt))]

    cbs = []
    for g in range(SSD_GROUPS):
        bg = xbc[:, SSD_WIDTH + g * SSD_STATE:SSD_WIDTH + (g + 1) * SSD_STATE].astype(BF16)
        cg = xbc[:, c_off + g * SSD_STATE:c_off + (g + 1) * SSD_STATE].astype(BF16)
        cbs.append((bg, cg, _dot_nt(cg, bg)))

    for p in range(n_pairs):
        bg, cg, cb = cbs[p // pairs_per_group]
        sel = sel_ref[:, 2 * LANES * p:2 * LANES * (p + 1)]
        fb = _dot(pieces[0], sel) + (_dot(pieces[1], sel) + _dot(pieces[2], sel))
        ms = []
        for i in range(2):
            h = 2 * p + i
            seg = fb[:L, i * LANES:(i + 1) * LANES] - acum_t[h:h + 1, :]
            dec = jnp.exp(jnp.where(causal, seg, -jnp.inf))
            ms.append((cb * dec).astype(BF16))
        ac = jnp.where(first, fb[:L, :LANES], fb[:L, LANES:])
        dtp = jnp.where(first, fb[L:, :LANES], fb[L:, LANES:])
        xp = xbc[:, p * LANES:(p + 1) * LANES]
        xdt = xp * dtp
        y = _dot(ms[0], jnp.where(first, xdt, 0.0).astype(BF16))
        y = y + _dot(ms[1], jnp.where(first, 0.0, xdt).astype(BF16))
        ht = ht_ref[bb * n_pairs + p]
        y = y + _dot(cg, ht.astype(BF16)) * jnp.exp(ac)
        a_last = ac[L - 1:L, :]
        ht_ref[bb * n_pairs + p] = (ht * jnp.exp(a_last)
                                    + _dot_tn(bg, (xdt * jnp.exp(a_last - ac)).astype(BF16)))
        y_ref[bb, :, p * LANES:(p + 1) * LANES] = y + xp * dsk_ref[:, p * LANES:(p + 1) * LANES]

    yg = y_ref[bb] * _silu(z_ref[bb].astype(F32))
    gw = SSD_WIDTH // SSD_GROUPS
    for g in range(SSD_GROUPS):
        blk = _rms(yg[:, g * gw:(g + 1) * gw]) * ng_ref[:, g * gw:(g + 1) * gw]
        o_ref[bb, :, g * gw:(g + 1) * gw] = blk.astype(o_ref.dtype)


def _ssd(proj3, dt3, conv_w, conv_b, dt_bias, a_log, d_skip, norm_g, nb=2):
    bsz, s, _ = proj3.shape
    L = SSD_CHUNK
    pad = LANES - SSD_HEADS
    dtb = jnp.pad(dt_bias.astype(F32), (0, pad)).reshape(1, LANES)
    alog = jnp.pad(a_log.astype(F32), (0, pad)).reshape(1, LANES)
    dsk = jnp.repeat(d_skip.astype(F32), SSD_HEAD_DIM).reshape(1, SSD_WIDTH)
    sel = (jnp.arange(LANES)[:, None] == (jnp.arange(SSD_HEADS * LANES)[None, :] // LANES)).astype(BF16)
    const = lambda shape: pl.BlockSpec(shape, lambda b, c: (0,) * len(shape))
    return pl.pallas_call(
        functools.partial(_ssd_kernel, nb=nb),
        grid=(bsz // nb, s // L),
        in_specs=[
            pl.BlockSpec((nb, L, SSD_WIDTH), lambda b, c: (b, c, 0)),
            pl.BlockSpec((nb, L, SSD_WIDTH), lambda b, c: (b, c, 1)),
            pl.BlockSpec((nb, L, 2 * SSD_STATE), lambda b, c: (b, c, 2 * SSD_WIDTH // (2 * SSD_STATE))),
            pl.BlockSpec((nb, L, 2 * SSD_STATE), lambda b, c: (b, c, 2 * SSD_WIDTH // (2 * SSD_STATE) + 1)),
            pl.BlockSpec((nb, L, LANES), lambda b, c: (b, c, 0)),
            const((SSD_CONV, XBC_WIDTH)), const((1, XBC_WIDTH)), const((1, LANES)), const((1, LANES)),
            const((1, SSD_WIDTH)), const((1, SSD_WIDTH)), const((LANES, SSD_HEADS * LANES)),
        ],
        out_specs=pl.BlockSpec((nb, L, SSD_WIDTH), lambda b, c: (b, c, 0)),
        out_shape=jax.ShapeDtypeStruct((bsz, s, SSD_WIDTH), BF16),
        scratch_shapes=[
            pltpu.VMEM((nb, L + 8, XBC_WIDTH), F32),
            pltpu.VMEM((nb * SSD_HEADS // 2, SSD_STATE, LANES), F32),
            pltpu.VMEM((nb, L, SSD_WIDTH), F32),
        ],
        compiler_params=_cparams(("parallel", "arbitrary")),
        name="ssd",
    )(proj3, proj3, proj3, proj3, dt3, conv_w.astype(F32), conv_b.astype(F32).reshape(1, XBC_WIDTH),
      dtb, alog, dsk, norm_g.astype(F32).reshape(1, SSD_WIDTH), sel)


def _half_norm(x, g):
    ra = lax.broadcasted_iota(I32, (LANES, LANES), 0) // DA_HALF
    rb = lax.broadcasted_iota(I32, (LANES, LANES), 1) // DA_HALF
    grp = jnp.where(ra == rb, 1.0, 0.0).astype(BF16)
    hi, mid = _split2(x * x)
    ss = _dot(hi, grp) + _dot(mid, grp)
    return x * lax.rsqrt(ss * (1.0 / DA_HALF) + EPS) * g


def _da_kernel(q_ref, k_ref, v_ref, gq_ref, gk_ref, lv_ref, sg_ref, o_ref,
               kn_ref, va_ref, q2_ref, m_ref, l_ref, acc_ref, *, lam_init, tq, seq):
    qi = pl.program_id(2)
    tk = tq

    @pl.when(qi == 0)
    def _():
        for c in range(seq // tk):
            rows = slice(c * tk, (c + 1) * tk)
            kn_ref[rows, :] = _half_norm(k_ref[0, rows, :].astype(F32), gk_ref[...]).astype(BF16)
            va_ref[rows, :LANES] = v_ref[0, rows, :]
            va_ref[rows, LANES:] = jnp.ones((tk, LANES), BF16)

    q = _half_norm(q_ref[0].astype(F32), gq_ref[...]).astype(BF16)
    first = lax.broadcasted_iota(I32, q.shape, 1) < DA_HALF
    zero = jnp.zeros_like(q)
    q2_ref[0] = jnp.where(first, q, zero)
    q2_ref[1] = jnp.where(first, zero, q)
    m_ref[...] = jnp.full(m_ref.shape, -jnp.inf, F32)
    l_ref[...] = jnp.zeros_like(l_ref)
    acc_ref[...] = jnp.zeros_like(acc_ref)

    def step(c, masked):
        k = kn_ref[c * tk:(c + 1) * tk, :]
        va = va_ref[c * tk:(c + 1) * tk, :]
        if masked:
            keep = (lax.broadcasted_iota(I32, (tq, tk), 1) <= lax.broadcasted_iota(I32, (tq, tk), 0))
        for j in range(2):
            s = _dot_nt(q2_ref[j], k)
            if masked:
                s = jnp.where(keep, s, -jnp.inf)
            m_prev = m_ref[j]
            m_new = jnp.maximum(m_prev, jnp.max(s, axis=-1, keepdims=True))
            alpha = jnp.exp(m_prev - m_new)
            p = jnp.exp(s - jnp.concatenate([m_new] * (tk // LANES), axis=1))
            pv = _dot(p.astype(BF16), va)
            l_ref[j] = alpha * l_ref[j] + pv[:, LANES:]
            acc_ref[j] = alpha * acc_ref[j] + pv[:, :LANES]
            m_ref[j] = m_new

    for qv in range(seq // tq):
        @pl.when(qi == qv)
        def _(qv=qv):
            for c in range(qv):
                step(c, False)
            step(qv, True)

    lv = lv_ref[...]
    lam = (jnp.exp(jnp.sum(lv[0:1] * lv[1:2], axis=-1, keepdims=True))
           - jnp.exp(jnp.sum(lv[2:3] * lv[3:4], axis=-1, keepdims=True)) + lam_init)
    o = acc_ref[0] / l_ref[0] - lam * (acc_ref[1] / l_ref[1])
    o_ref[0] = (_rms(o) * (sg_ref[...] * (1.0 - lam_init))).astype(o_ref.dtype)


def _diff_attn(proj3, gq, gk, lvec, subln_g, lam_init, tq=512):
    bsz, s, _ = proj3.shape
    const = lambda shape: pl.BlockSpec(shape, lambda b, h, qi: (0,) * len(shape))
    return pl.pallas_call(
        functools.partial(_da_kernel, lam_init=lam_init, tq=tq, seq=s),
        grid=(bsz, DA_HEADS, s // tq),
        in_specs=[
            pl.BlockSpec((1, tq, LANES), lambda b, h, qi: (b, qi, COL_Q + h)),
            pl.BlockSpec((1, s, LANES), lambda b, h, qi: (b, 0, COL_Q + DA_HEADS + h)),
            pl.BlockSpec((1, s, LANES), lambda b, h, qi: (b, 0, COL_V + h)),
            const((1, LANES)), const((1, LANES)), const((4, DA_HALF)), const((1, LANES)),
        ],
        out_specs=pl.BlockSpec((1, tq, LANES), lambda b, h, qi: (b, qi, h)),
        out_shape=jax.ShapeDtypeStruct((bsz, s, DA_WIDTH), BF16),
        scratch_shapes=[
            pltpu.VMEM((s, LANES), BF16),
            pltpu.VMEM((s, 2 * LANES), BF16),
            pltpu.VMEM((2, tq, LANES), BF16),
            pltpu.VMEM((2, tq, LANES), F32),
            pltpu.VMEM((2, tq, LANES), F32),
            pltpu.VMEM((2, tq, LANES), F32),
        ],
        compiler_params=_cparams(("parallel", "parallel", "arbitrary")),
        name="diff_attn",
    )(proj3, proj3, proj3, gq, gk, lvec, subln_g)


def _xattn_kernel(q_ref, k_ref, v_ref, o_ref):
    s = _dot_nt(q_ref[...], k_ref[...])
    p = jnp.exp(s - jnp.max(s, axis=-1, keepdims=True))
    o = _dot(p.astype(BF16), v_ref[...]) / jnp.sum(p, axis=-1, keepdims=True)
    o_ref[...] = o.astype(o_ref.dtype)


def _xattn(qx, kv, seq, mem_len, tm=512):
    m = qx.shape[0]
    hd = XA_HEAD_DIM
    return pl.pallas_call(
        _xattn_kernel,
        grid=(m // tm, XA_HEADS),
        in_specs=[
            pl.BlockSpec((tm, hd), lambda i, j: (i, j)),
            pl.BlockSpec((mem_len, hd), lambda i, j: ((i * tm) // seq, j)),
            pl.BlockSpec((mem_len, hd), lambda i, j: ((i * tm) // seq, XA_HEADS + j)),
        ],
        out_specs=pl.BlockSpec((tm, hd), lambda i, j: (i, j)),
        out_shape=jax.ShapeDtypeStruct((m, XA_HEADS * hd), BF16),
        compiler_params=_cparams(("parallel", "parallel")),
        name="xattn",
    )(qx, kv, kv)


def _router_kernel(x_ref, g_ref, rw_ref, rb_ref, hf_ref, route_ref, gate_ref, cnt_ref, carry_ref, *, tm):
    i = pl.program_id(0)

    @pl.when(i == 0)
    def _():
        carry_ref[...] = jnp.zeros_like(carry_ref)

    hn = _rms(x_ref[...]) * g_ref[...]
    hi, mid = _split2(hn)
    hf_ref[...] = hi.reshape(tm, ROW_SUB, LANES)

    logits = _dot(hi, rw_ref[0]) + (_dot(hi, rw_ref[1]) + _dot(mid, rw_ref[0])) + rb_ref[...]
    lane = lax.broadcasted_iota(I32, (tm, LANES), 1)
    lane_f = lane.astype(F32)
    idxs, vals = [], []
    work = logits
    for _ in range(TOP_K):
        m = jnp.max(work, axis=-1, keepdims=True)
        ik = jnp.min(jnp.where(work == m, lane_f, float(LANES)), axis=-1, keepdims=True)
        idxs.append(ik)
        vals.append(m)
        work = jnp.where(lane_f == ik, -jnp.inf, work)
    es = [jnp.exp(v - vals[0]) for v in vals]
    den = es[0] + es[1] + es[2] + es[3]

    onehot = jnp.zeros((tm, LANES), F32)
    for ik in idxs:
        onehot = onehot + jnp.where(lane_f == ik, 1.0, 0.0)
    r = lax.broadcasted_iota(I32, (tm, tm), 0)
    c = lax.broadcasted_iota(I32, (tm, tm), 1)
    strict = jnp.where(c < r, 1.0, 0.0).astype(BF16)
    before = carry_ref[0:1, :] + _dot(strict, onehot.astype(BF16))
    carry_ref[0:1, :] = carry_ref[0:1, :] + jnp.sum(onehot, axis=0, keepdims=True)
    cnt_ref[...] = jnp.broadcast_to(carry_ref[0:1, :], cnt_ref.shape)

    route = jnp.zeros((tm, LANES), F32)
    gates = jnp.zeros((tm, LANES), F32)
    for kk in range(TOP_K):
        rank = jnp.sum(jnp.where(lane_f == idxs[kk], before, 0.0), axis=-1, keepdims=True)
        route = jnp.where(lane == kk, idxs[kk], route)
        route = jnp.where(lane == TOP_K + kk, rank, route)
        gates = jnp.where(lane == kk, es[kk] / den, gates)
    route_ref[...] = route.astype(I32)
    gate_ref[...] = gates


def _router(h2, g, rw3, rb, tm=512):
    t, d = h2.shape
    return pl.pallas_call(
        functools.partial(_router_kernel, tm=tm),
        grid=(t // tm,),
        in_specs=[
            pl.BlockSpec((tm, d), lambda i: (i, 0)),
            pl.BlockSpec((1, d), lambda i: (0, 0)),
            pl.BlockSpec((3, d, LANES), lambda i: (0, 0, 0)),
            pl.BlockSpec((1, LANES), lambda i: (0, 0)),
        ],
        out_specs=[
            pl.BlockSpec((tm, ROW_SUB, LANES), lambda i: (i, 0, 0)),
            pl.BlockSpec((tm, LANES), lambda i: (i, 0)),
            pl.BlockSpec((tm, LANES), lambda i: (i, 0)),
            pl.BlockSpec((8, LANES), lambda i: (0, 0)),
        ],
        out_shape=[
            jax.ShapeDtypeStruct((t, ROW_SUB, LANES), BF16),
            jax.ShapeDtypeStruct((t, LANES), I32),
            jax.ShapeDtypeStruct((t, LANES), F32),
            jax.ShapeDtypeStruct((8, LANES), F32),
        ],
        scratch_shapes=[pltpu.VMEM((8, LANES), F32)],
        compiler_params=_cparams(("arbitrary",)),
        name="router",
    )(h2, g, rw3, rb)


def _dispatch_kernel(dest_ref, nr_ref, hf_ref, xs_ref, zero_ref, sem, zsem, *, tt, n_blocks):
    @pl.when(pl.program_id(0) == 0)
    def _():
        zero_ref[...] = jnp.zeros_like(zero_ref)

        def fill(start):
            def body(b, carry):
                pad = MOE_TM - nr_ref[b]
                off = b * MOE_TM + nr_ref[b]
                size = MOE_TM
                while size >= 1:
                    hit = (pad & size) != 0
                    cp = pltpu.make_async_copy(zero_ref.at[pl.ds(0, size)], xs_ref.at[pl.ds(off, size)], zsem)

                    @pl.when(hit)
                    def _(cp=cp):
                        cp.start() if start else cp.wait()

                    off = off + jnp.where(hit, size, 0)
                    size //= 2
                return carry
            lax.fori_loop(0, n_blocks, body, 0)

        fill(True)
        fill(False)

    def copy(t, kk):
        return pltpu.make_async_copy(hf_ref.at[t], xs_ref.at[dest_ref[0, 0, t * TOP_K + kk]], sem)

    def issue(t, carry):
        for kk in range(TOP_K):
            copy(t, kk).start(priority=kk % 2)
        return carry

    def drain(t, carry):
        for kk in range(TOP_K):
            copy(t, kk).wait()
        return carry

    lax.fori_loop(0, tt, issue, 0)
    lax.fori_loop(0, tt, drain, 0)


def _dispatch(dest2, nr, hf3, n_blocks, tt):
    t = hf3.shape[0]
    return pl.pallas_call(
        functools.partial(_dispatch_kernel, tt=tt, n_blocks=n_blocks),
        grid=(t // tt,),
        in_specs=[
            pl.BlockSpec((1, 1, tt * TOP_K), lambda i: (i, 0, 0), memory_space=pltpu.SMEM),
            pl.BlockSpec(memory_space=pltpu.SMEM),
            pl.BlockSpec((tt, ROW_SUB, LANES), lambda i: (i, 0, 0)),
        ],
        out_specs=pl.BlockSpec(memory_space=pl.ANY),
        out_shape=jax.ShapeDtypeStruct((n_blocks * MOE_TM, ROW_SUB, LANES), BF16),
        scratch_shapes=[pltpu.VMEM((MOE_TM, ROW_SUB, LANES), BF16), pltpu.SemaphoreType.DMA(()),
                        pltpu.SemaphoreType.DMA(())],
        compiler_params=_cparams(("arbitrary",)),
        name="dispatch",
    )(dest2, nr, hf3)


def _stream_expert_weights(meta_refs, i, copies):
    be_ref, _, _, first_ref, seg_ref, nxt_ref, nseg_ref = meta_refs
    k = seg_ref[i]
    slot = lax.rem(k, 2)

    @pl.when(first_ref[i] == 1)
    def _():
        @pl.when(k == 0)
        def _():
            for c in copies(be_ref[i], slot):
                c.start()

        @pl.when(k < nseg_ref[0] - 1)
        def _():
            for c in copies(nxt_ref[i], 1 - slot):
                c.start()

        for c in copies(be_ref[i], slot):
            c.wait()

    return slot


def _dot_w(x, w):
    return lax.dot_general(x, w, (((1,), (0,)), ((), ())), preferred_element_type=F32)


def _block_cases(nrows):
    q = MOE_TM // 4
    return [
        (nrows > 3 * q, [(0, MOE_TM)], MOE_TM),
        (jnp.logical_and(nrows > 2 * q, nrows <= 3 * q), [(0, 2 * q), (2 * q, q)], 3 * q),
        (jnp.logical_and(nrows > q, nrows <= 2 * q), [(0, 2 * q)], 2 * q),
        (jnp.logical_and(nrows > 0, nrows <= q), [(0, q)], q),
        (nrows <= 0, [], 0),
    ]


def _moe_up_kernel(be_ref, bx_ref, nr_ref, first_ref, seg_ref, nxt_ref, nseg_ref, x_ref, w_hbm, bg_ref,
                   bu_ref, o_ref, wbuf_ref, sem):
    f = pl.program_id(0)
    i = pl.program_id(1)
    nrows = nr_ref[i]
    nf = D_FF // MOE_TF

    def copies(e, slot):
        out = []
        for part in range(2):
            col = pl.multiple_of((part * nf + f) * MOE_TF, MOE_TF)
            out.append(pltpu.make_async_copy(w_hbm.at[e, :, pl.ds(col, MOE_TF)], wbuf_ref.at[slot, part],
                                             sem.at[slot, part]))
        return out

    slot = _stream_expert_weights((be_ref, bx_ref, nr_ref, first_ref, seg_ref, nxt_ref, nseg_ref), i, copies)

    for cond, pieces, dead in _block_cases(nrows):
        @pl.when(cond)
        def _(pieces=pieces, dead=dead):
            for start, size in pieces:
                rows = slice(start, start + size)
                xb = x_ref[rows].reshape(size, D_MODEL)
                g = jnp.minimum(_dot_w(xb, wbuf_ref[slot, 0]) + bg_ref[0], SWIGLU_LIMIT)
                u = jnp.clip(_dot_w(xb, wbuf_ref[slot, 1]) + bu_ref[0], -SWIGLU_LIMIT, SWIGLU_LIMIT)
                act = g * jax.nn.sigmoid(SWIGLU_ALPHA * g) * (u + 1.0)
                o_ref[rows, :] = act.astype(o_ref.dtype)
            if dead < MOE_TM:
                o_ref[dead:, :] = jnp.zeros((MOE_TM - dead, MOE_TF), o_ref.dtype)


def _moe_up(meta, xs3, w1, b1, n_blocks):
    n_slots = xs3.shape[0]
    nf = D_FF // MOE_TF
    grid_spec = pltpu.PrefetchScalarGridSpec(
        num_scalar_prefetch=len(meta),
        grid=(nf, n_blocks),
        in_specs=[
            pl.BlockSpec((MOE_TM, ROW_SUB, LANES), lambda f, i, be, bx, *_: (bx[i], 0, 0)),
            pl.BlockSpec(memory_space=pl.ANY),
            pl.BlockSpec((1, 1, MOE_TF), lambda f, i, be, bx, *_: (be[i], 0, f)),
            pl.BlockSpec((1, 1, MOE_TF), lambda f, i, be, bx, *_: (be[i], 0, nf + f)),
        ],
        out_specs=pl.BlockSpec((MOE_TM, MOE_TF), lambda f, i, *_: (i, f)),
        scratch_shapes=[
            pltpu.VMEM((2, 2, D_MODEL, MOE_TF), F32),
            pltpu.SemaphoreType.DMA((2, 2)),
        ],
    )
    return pl.pallas_call(
        _moe_up_kernel,
        grid_spec=grid_spec,
        out_shape=jax.ShapeDtypeStruct((n_slots, D_FF), BF16),
        compiler_params=_cparams(("arbitrary", "arbitrary")),
        name="moe_up",
    )(*meta, xs3, w1, b1, b1)


def _moe_down_kernel(be_ref, bx_ref, nr_ref, first_ref, seg_ref, nxt_ref, nseg_ref, a_ref, w_hbm, b_ref,
                     o_ref, wbuf_ref, sem):
    i = pl.program_id(0)
    nrows = nr_ref[i]

    def copies(e, slot):
        return [pltpu.make_async_copy(w_hbm.at[e], wbuf_ref.at[slot], sem.at[slot])]

    slot = _stream_expert_weights((be_ref, bx_ref, nr_ref, first_ref, seg_ref, nxt_ref, nseg_ref), i, copies)

    for cond, pieces, dead in _block_cases(nrows):
        @pl.when(cond)
        def _(pieces=pieces, dead=dead):
            for start, size in pieces:
                rows = slice(start, start + size)
                y = _dot_w(a_ref[rows, :], wbuf_ref[slot]) + b_ref[0]
                o_ref[rows] = y.astype(o_ref.dtype).reshape(size, ROW_SUB, LANES)
            if dead < MOE_TM:
                o_ref[dead:] = jnp.zeros((MOE_TM - dead, ROW_SUB, LANES), o_ref.dtype)


def _moe_down(meta, act, w2, b2, n_blocks):
    n_slots = act.shape[0]
    grid_spec = pltpu.PrefetchScalarGridSpec(
        num_scalar_prefetch=len(meta),
        grid=(n_blocks,),
        in_specs=[
            pl.BlockSpec((MOE_TM, D_FF), lambda i, be, bx, *_: (bx[i], 0)),
            pl.BlockSpec(memory_space=pl.ANY),
            pl.BlockSpec((1, 1, D_MODEL), lambda i, be, bx, *_: (be[i], 0, 0)),
        ],
        out_specs=pl.BlockSpec((MOE_TM, ROW_SUB, LANES), lambda i, *_: (i, 0, 0)),
        scratch_shapes=[
            pltpu.VMEM((2, D_FF, D_MODEL), F32),
            pltpu.SemaphoreType.DMA((2,)),
        ],
    )
    return pl.pallas_call(
        _moe_down_kernel,
        grid_spec=grid_spec,
        out_shape=jax.ShapeDtypeStruct((n_slots, ROW_SUB, LANES), BF16),
        compiler_params=_cparams(("arbitrary",)),
        name="moe_down",
    )(*meta, act, w2, b2)


def _combine_kernel(tab_ref, tabn_ref, pos_ref, gate_ref, y_ref, h_ref, o_ref, buf_ref, stage_ref, sem, *, tc):
    i = pl.program_id(0)
    slot = lax.rem(i, 2)

    def chunks(t_ref, s, start):
        def per_chunk(c, carry):
            cp = pltpu.make_async_copy(y_ref.at[pl.ds(t_ref[0, 0, 1 + c], COMBINE_CHUNK)],
                                       buf_ref.at[s, pl.ds(c * COMBINE_CHUNK, COMBINE_CHUNK)], sem.at[s])
            cp.start() if start else cp.wait()
            return carry

        lax.fori_loop(0, t_ref[0, 0, 0], per_chunk, 0)

    @pl.when(i == 0)
    def _():
        chunks(tab_ref, slot, True)

    @pl.when(i + 1 < pl.num_programs(0))
    def _():
        chunks(tabn_ref, 1 - slot, True)

    chunks(tab_ref, slot, False)

    def token(t, carry):
        acc = jnp.zeros((ROW_SUB, LANES), F32)
        for kk in range(TOP_K):
            j = t * TOP_K + kk
            acc = acc + gate_ref[0, 0, j] * buf_ref[slot, pos_ref[0, 0, j]].astype(F32)
        stage_ref[t] = acc
        return carry

    lax.fori_loop(0, tc, token, 0)
    o_ref[...] = h_ref[...] + stage_ref[...].reshape(tc, D_MODEL)


def _combine(tables, pos3, gate3, y3, h2, tc):
    t, d = h2.shape
    nt = t // tc
    n_tab = tables.shape[2]
    buf_rows = (n_tab - 1) * COMBINE_CHUNK
    smem = lambda shape, nxt: pl.BlockSpec(
        shape, (lambda i: (jnp.minimum(i + 1, nt - 1), 0, 0)) if nxt else (lambda i: (i, 0, 0)),
        memory_space=pltpu.SMEM)
    return pl.pallas_call(
        functools.partial(_combine_kernel, tc=tc),
        grid=(nt,),
        in_specs=[
            smem((1, 1, n_tab), False), smem((1, 1, n_tab), True),
            smem((1, 1, tc * TOP_K), False), smem((1, 1, tc * TOP_K), False),
            pl.BlockSpec(memory_space=pl.ANY),
            pl.BlockSpec((tc, d), lambda i: (i, 0)),
        ],
        out_specs=pl.BlockSpec((tc, d), lambda i: (i, 0)),
        out_shape=jax.ShapeDtypeStruct((t, d), F32),
        scratch_shapes=[
            pltpu.VMEM((2, buf_rows, ROW_SUB, LANES), BF16),
            pltpu.VMEM((tc, ROW_SUB, LANES), F32),
            pltpu.SemaphoreType.DMA((2,)),
        ],
        compiler_params=_cparams(("arbitrary",)),
        name="combine",
    )(tables, tables, pos3, gate3, y3, h2)


def _combine_plan(route, gates, slot_start, tc):
    t = route.shape[0]
    nt = t // tc
    idx = route[:, :TOP_K].reshape(nt, 1, tc * TOP_K)
    rank = route[:, TOP_K:2 * TOP_K].reshape(nt, tc * TOP_K)
    hit = idx == jnp.arange(N_EXPERTS, dtype=I32)[None, :, None]
    in_tile = jnp.sum(hit.astype(I32), axis=2)
    before = jnp.cumsum(in_tile, axis=0) - in_tile
    n_chunks = (in_tile + COMBINE_CHUNK - 1) // COMBINE_CHUNK
    c_end = jnp.cumsum(n_chunks, axis=1)
    c_start = c_end - n_chunks
    off = COMBINE_CHUNK * c_start
    pos = jnp.sum(jnp.where(hit, (off - before)[:, :, None], 0), axis=1) + rank
    max_chunks = (tc * TOP_K + N_EXPERTS * (COMBINE_CHUNK - 1)) // COMBINE_CHUNK
    c_ids = jnp.arange(max_chunks, dtype=I32)
    owner = jnp.logical_and(c_start[:, :, None] <= c_ids, c_ids < c_end[:, :, None])
    first = slot_start[None, :] + before - off
    src = jnp.sum(jnp.where(owner, first[:, :, None], 0), axis=1) + COMBINE_CHUNK * c_ids
    tables = jnp.concatenate([c_end[:, -1:], src], axis=1).astype(I32)
    gate3 = gates[:, :TOP_K].reshape(nt, 1, tc * TOP_K)
    return tables.reshape(nt, 1, 1 + max_chunks), pos.astype(I32).reshape(nt, 1, -1), gate3


def _moe_plan(counts, route, n_blocks):
    idx = route[:, :TOP_K].reshape(-1)
    rank = route[:, TOP_K:2 * TOP_K].reshape(-1)
    nblk = (counts + MOE_TM - 1) // MOE_TM
    blk_end = jnp.cumsum(nblk)
    blk_start = blk_end - nblk
    dest = (blk_start * MOE_TM)[idx] + rank
    total = blk_end[-1]
    ids = jnp.arange(n_blocks, dtype=I32)
    live = ids < total
    src = jnp.minimum(ids, total - 1)
    be = jnp.minimum(jnp.sum(blk_end[None, :] <= src[:, None], axis=1), N_EXPERTS - 1).astype(I32)
    nr = jnp.where(live, jnp.clip(counts[be] - (ids - blk_start[be]) * MOE_TM, 0, MOE_TM), 0).astype(I32)
    first = jnp.logical_and(live, ids == blk_start[be]).astype(I32)
    has = counts > 0
    ordinal = jnp.cumsum(has.astype(I32)) - 1
    e_ids = jnp.arange(N_EXPERTS, dtype=I32)
    later = jnp.logical_and(has[None, :], e_ids[None, :] > e_ids[:, None])
    nxt_e = jnp.min(jnp.where(later, e_ids[None, :], N_EXPERTS), axis=1)
    nxt_e = jnp.where(nxt_e == N_EXPERTS, jnp.argmax(has).astype(I32), nxt_e)
    nseg = jnp.sum(has.astype(I32)).reshape(1)
    meta = (be, src.astype(I32), nr, first, ordinal[be].astype(I32), nxt_e[be].astype(I32), nseg)
    return dest.astype(I32), meta, (blk_start * MOE_TM).astype(I32)


def _layer(h, mem, lam_init, norm_mix_g, w_in, conv_w, conv_b, dt_bias, a_log, d_skip, ssd_norm_g,
           da_q_norm_g, da_k_norm_g, lq1, lk1, lq2, lk2, da_subln_g, w_out, norm_xa_g, norm_mem_g,
           xa_wq, xa_wkv, xa_q_norm_g, xa_k_norm_g, xa_wo, norm_ffn_g, router_w, router_b,
           moe_w1, moe_b1, moe_w2, moe_b2):
    bsz, seq, d = h.shape
    t = bsz * seq
    mem_len = mem.shape[1]
    row = lambda v: v.astype(F32).reshape(1, -1)
    x2 = h.reshape(t, d)

    w_t = jnp.swapaxes(w_in, 0, 1)
    w_main = _w_prep(w_t)
    w_dt = _split3_host(jnp.pad(jnp.swapaxes(_w_dt_rows(w_t), 0, 1), ((0, 0), (0, LANES - SSD_HEADS))))
    proj, dt_raw = _in_proj(x2, row(norm_mix_g), w_main, w_dt)
    proj3 = proj.reshape(bsz, seq, PROJ_COLS)
    y_ssd = _ssd(proj3, dt_raw.reshape(bsz, seq, LANES), conv_w, conv_b, dt_bias, a_log, d_skip, ssd_norm_g)

    gq = row(jnp.tile(da_q_norm_g.astype(F32), 2) * (DA_HALF ** -0.5))
    gk = row(jnp.tile(da_k_norm_g.astype(F32), 2))
    lvec = jnp.stack([lq1, lk1, lq2, lk2]).astype(F32)
    y_da = _diff_attn(proj3, gq, gk, lvec, row(da_subln_g), lam_init)

    h1 = _mm_res([y_ssd.reshape(t, SSD_WIDTH), y_da.reshape(t, DA_WIDTH)], w_out, x2)

    eq = jnp.tile(xa_q_norm_g.astype(F32) * (XA_HEAD_DIM ** -0.5), XA_HEADS).reshape(1, -1)
    qx = _norm_proj(h1, row(norm_xa_g), xa_wq.astype(BF16), eq[:, :XA_HEAD_DIM], XA_HEADS, 1024, XA_HEAD_DIM)
    kv = _norm_proj(mem.reshape(bsz * mem_len, d), row(norm_mem_g), xa_wkv.astype(BF16),
                    row(xa_k_norm_g), XA_HEADS, bsz * mem_len, XA_HEAD_DIM)
    ox = _xattn(qx, kv, seq, mem_len)
    h2 = _mm_res([ox], xa_wo, h1)

    rw3 = _split3_host(jnp.pad(router_w, ((0, 0), (0, LANES - N_EXPERTS))))
    rb = jnp.pad(router_b.astype(F32), (0, LANES - N_EXPERTS), constant_values=-jnp.inf).reshape(1, LANES)
    hf, route, gates, cnt = _router(h2, row(norm_ffn_g), rw3, rb)
    n_blocks = (t * TOP_K) // MOE_TM + N_EXPERTS + 1
    n_slots = n_blocks * MOE_TM
    dest, meta, slot_start = _moe_plan(cnt[0, :N_EXPERTS].astype(I32), route, n_blocks)
    tt = 512
    xs = _dispatch(dest.reshape(t // tt, 1, tt * TOP_K), meta[2], hf, n_blocks, tt)
    act = _moe_up(meta, xs, moe_w1,
                  moe_b1.reshape(N_EXPERTS, 1, 2 * D_FF), n_blocks)
    y = _moe_down(meta, act, moe_w2, moe_b2.reshape(N_EXPERTS, 1, D_MODEL), n_blocks)
    tc = 256
    out = _combine(*_combine_plan(route, gates, slot_start, tc), y, h2, tc)
    return out.reshape(bsz, seq, d)


def kernel(x, mem, norm_mix_g, w_in, conv_w, conv_b, dt_bias, a_log, d_skip, ssd_norm_g, da_q_norm_g,
           da_k_norm_g, lambda_q1, lambda_k1, lambda_q2, lambda_k2, da_subln_g, w_out, norm_xa_g,
           norm_mem_g, xa_wq, xa_wkv, xa_q_norm_g, xa_k_norm_g, xa_wo, norm_ffn_g, router_w, router_b,
           moe_w1, moe_b1, moe_w2, moe_b2):
    h = x
    for layer in range(norm_mix_g.shape[0]):
        lam_init = 0.8 - 0.6 * math.exp(-0.3 * layer)
        h = _layer(h, mem, lam_init, norm_mix_g[layer], w_in[layer], conv_w[layer], conv_b[layer],
                   dt_bias[layer], a_log[layer], d_skip[layer], ssd_norm_g[layer], da_q_norm_g[layer],
                   da_k_norm_g[layer], lambda_q1[layer], lambda_k1[layer], lambda_q2[layer],
                   lambda_k2[layer], da_subln_g[layer], w_out[layer], norm_xa_g[layer], norm_mem_g[layer],
                   xa_wq[layer], xa_wkv[layer], xa_q_norm_g[layer], xa_k_norm_g[layer], xa_wo[layer],
                   norm_ffn_g[layer], router_w[layer], router_b[layer], moe_w1[layer], moe_b1[layer],
                   moe_w2[layer], moe_b2[layer])
    return h
```
